```python
import math
import jax, jax.numpy as jnp
from jax import lax
import numpy as np

D_MODEL = 1024
BATCH = 8
SEQ = 2048
DEPTH = 4
DEC_BATCH = 128
DEC_SEQ = 1
PAST_LEN = 2048
PAGE_SIZE = 128

H_A = 8
HD_A = 64
W_A = H_A * HD_A
LORA_W = 64
LORA_A = 64
LORA_G = 128
GN_EPS_A = 64e-5
H_B = 4
HD_B = 64
W_B = H_B * 2 * HD_B
N_BUCKETS = 32
MAX_DISTANCE = 128
Q_BLOCK = 128
SUBLN_EPS = 1e-5
H_C = 8
HD_C = 64
W_C = H_C * HD_C
CONV_W = 4
CHUNK = 64
N_MEM = 256
H_MEM = 4
HD_MEM = 64
W_MEM = H_MEM * HD_MEM
D_FF = 4 * D_MODEL
RMS_EPS = 1e-6
A_COLS = 3 * W_A + LORA_W + LORA_A + LORA_G
B_COLS = 3 * W_B
C_COLS = 4 * W_C + 2 * H_C
G_COLS = 3 * D_MODEL
IN_COLS = A_COLS + B_COLS + C_COLS + G_COLS

kernel_name = 'hybrid_rwkv7_diffattn_gdn_step'


def _split(x, sizes):
    cuts = [int(c) for c in np.cumsum(sizes)[:-1]]
    return jnp.split(x, cuts, axis=-1)


def rmsnorm(x, g, eps=RMS_EPS):
    xf = x.astype(jnp.float32)
    y = xf * lax.rsqrt(jnp.mean(xf * xf, axis=-1, keepdims=True) + eps)
    return (y * g.astype(jnp.float32)).astype(x.dtype)


def l2norm(x, eps=1e-6):
    return x * lax.rsqrt(jnp.sum(x * x, axis=-1, keepdims=True) + eps)


def t5_bucket(dist):
    n = jnp.maximum(dist, 0)
    max_exact = N_BUCKETS // 2
    nf = jnp.maximum(n, 1).astype(jnp.float32)
    large = max_exact + (jnp.log(nf / max_exact) / math.log(MAX_DISTANCE / max_exact)
                         * (N_BUCKETS - max_exact)).astype(jnp.int32)
    return jnp.where(n < max_exact, n, jnp.minimum(large, N_BUCKETS - 1))


def rwkv7_scan(r, w, k, v, a, b, s0):
    def step(S, inp):
        rt, wt, kt, vt, at, bt = inp
        sa = jnp.einsum('bhvk,bhk->bhv', S, at)
        S = S * wt[:, :, None, :] + sa[..., :, None] * bt[..., None, :] + vt[..., :, None] * kt[..., None, :]
        return S, jnp.einsum('bhvk,bhk->bhv', S, rt)
    xs = tuple(t.transpose(1, 0, 2, 3) for t in (r, w, k, v, a, b))
    s_final, ys = lax.scan(step, s0, xs)
    return ys.transpose(1, 0, 2, 3), s_final


def gated_delta_chunked(q, k, v, g, beta, s0):
    nb, T, H, dk = q.shape
    dv = v.shape[-1]
    L = min(CHUNK, T)
    pad = (-T) % L
    if pad:
        pw = ((0, 0), (0, pad), (0, 0), (0, 0))
        q, k, v = jnp.pad(q, pw), jnp.pad(k, pw), jnp.pad(v, pw)
        g, beta = jnp.pad(g, pw[:3]), jnp.pad(beta, pw[:3])
    n = (T + pad) // L
    c4 = lambda t: t.reshape(nb, n, L, H, t.shape[-1]).transpose(1, 0, 3, 2, 4)
    c3 = lambda t: t.reshape(nb, n, L, H).transpose(1, 0, 3, 2)
    qc, kc, vc = c4(q), c4(k), c4(v)
    gc = jnp.cumsum(c3(g), axis=-1)
    bc = c3(beta)
    tri = jnp.tril(jnp.ones((L, L), dtype=bool))
    strict = jnp.tril(jnp.ones((L, L), dtype=bool), -1)
    decay = jnp.exp(jnp.where(tri, gc[..., :, None] - gc[..., None, :], -jnp.inf))
    kb = kc * bc[..., None]
    lmat = jnp.where(strict, jnp.einsum('nbhid,nbhjd->nbhij', kb, kc) * decay, 0.0)
    eye = jnp.eye(L, dtype=jnp.float32)
    tmat = lax.linalg.triangular_solve(eye + lmat, jnp.broadcast_to(eye, lmat.shape), left_side=True, lower=True)
    u = jnp.einsum('nbhij,nbhje->nbhie', tmat, vc * bc[..., None])
    wk = jnp.einsum('nbhij,nbhjd->nbhid', tmat, kb * jnp.exp(gc)[..., None])

    def step(S, inp):
        qi, ki, ui, wi, gi, di = inp
        attn = jnp.where(tri, jnp.einsum('bhid,bhjd->bhij', qi, ki) * di, 0.0)
        v_new = ui - jnp.einsum('bhld,bhde->bhle', wi, S)
        o = jnp.einsum('bhld,bhde->bhle', qi * jnp.exp(gi)[..., None], S) + jnp.einsum('bhij,bhje->bhie', attn, v_new)
        glast = gi[..., -1]
        S = S * jnp.exp(glast)[..., None, None] + jnp.einsum(
            'bhld,bhle->bhde', ki * jnp.exp(glast[..., None] - gi)[..., None], v_new)
        return S, o

    s_final, o = lax.scan(step, s0, (qc, kc, u, wk, gc, decay))
    o = o.transpose(1, 0, 3, 2, 4).reshape(nb, n * L, H, dv)[:, :T]
    return o, s_final


def diff_attention(q, k, v, q_pos, k_pos, rel_bias, lam):
    nb, Tq = q.shape[:2]
    qb = Q_BLOCK if Tq % Q_BLOCK == 0 else Tq
    n_blk = Tq // qb
    scale = HD_B ** -0.5
    qs = q.reshape(nb, n_blk, qb, H_B, 2, HD_B).transpose(1, 0, 2, 3, 4, 5)
    ps = q_pos.reshape(n_blk, qb)
    k2 = k.reshape(nb, k.shape[1], H_B, 2, HD_B)

    def block(args):
        qblk, pblk = args
        s = jnp.einsum('bqhcd,bkhcd->bchqk', qblk, k2).astype(jnp.float32) * scale
        dist = pblk[:, None] - k_pos[None, :]
        bias = rel_bias[t5_bucket(dist)].astype(jnp.float32).transpose(2, 0, 1)
        s = jnp.where((dist >= 0)[None, None, None], s + bias[None, None], -1e30)
        pr = jax.nn.softmax(s, axis=-1)
        amap = pr[:, 0] - lam * pr[:, 1]
        return jnp.einsum('bhqk,bkhe->bqhe', amap.astype(v.dtype), v)

    out = lax.map(block, (qs, ps))
    return out.transpose(1, 0, 2, 3, 4).reshape(nb, Tq, H_B, 2 * HD_B)


def parallel_mixer(p, l, h, a_shift, a_wkv, c_conv, c_delta, b_k_past, b_v_past):
    nb, T, _ = h.shape
    dt = h.dtype
    f32 = jnp.float32
    proj = h @ p['w_in'][l]
    pa, pb, pc, pg = _split(proj, [A_COLS, B_COLS, C_COLS, G_COLS])

    prev = jnp.concatenate([a_shift[:, None, :].astype(dt), pa[:, :-1]], axis=1)
    xa = pa + (prev - pa) * p['a_mu'][l]
    new_a_shift = pa[:, -1]
    ar, ak, av, aw, aa, ag = _split(xa, [W_A, W_A, W_A, LORA_W, LORA_A, LORA_G])
    heads_a = lambda t: t.reshape(nb, T, H_A, HD_A)
    w_raw = (p['a_w0'][l] + jnp.tanh(aw) @ p['a_w2'][l]).astype(f32)
    decay = jnp.exp(-jnp.exp(-jax.nn.softplus(-w_raw) - 0.5))
    a_in = jax.nn.sigmoid((p['a_a0'][l] + aa @ p['a_a2'][l]).astype(f32))
    g_out = jax.nn.sigmoid(ag) @ p['a_g2'][l]
    kf = ak.astype(f32)
    kk = l2norm(heads_a(kf * p['a_kk'][l]), 1e-12)
    kf = heads_a(kf * (1.0 + (a_in - 1.0) * p['a_ka'][l]))
    rf, vf, a_h = heads_a(ar.astype(f32)), heads_a(av.astype(f32)), heads_a(a_in)
    ya, new_a_wkv = rwkv7_scan(rf, heads_a(decay), kf, vf, -kk, kk * a_h, a_wkv.astype(f32))
    mu = jnp.mean(ya, axis=-1, keepdims=True)
    var = jnp.mean(jnp.square(ya - mu), axis=-1, keepdims=True)
    ya = ((ya - mu) * lax.rsqrt(var + GN_EPS_A)).reshape(nb, T, W_A) * p['a_ln_w'][l] + p['a_ln_b'][l]
    bonus = jnp.sum(rf * kf * p['a_rk'][l], axis=-1, keepdims=True) * vf
    ya = (ya + bonus.reshape(nb, T, W_A)).astype(dt) * g_out

    bq, bk, bv = (t.reshape(nb, T, H_B, 2 * HD_B) for t in _split(pb, [W_B, W_B, W_B]))
    if b_k_past is None:
        keys, vals, past = bk, bv, 0
    else:
        keys = jnp.concatenate([b_k_past.astype(dt), bk], axis=1)
        vals = jnp.concatenate([b_v_past.astype(dt), bv], axis=1)
        past = b_k_past.shape[1]
    lam_init = 0.8 - 0.6 * math.exp(-0.3 * l)
    lam = (jnp.exp(jnp.sum((p['b_lam_q1'][l] * p['b_lam_k1'][l]).astype(f32)))
           - jnp.exp(jnp.sum((p['b_lam_q2'][l] * p['b_lam_k2'][l]).astype(f32))) + lam_init)
    yb = diff_attention(bq, keys, vals, past + jnp.arange(T), jnp.arange(past + T), p['rel_bias'], lam)
    yb = (rmsnorm(yb, p['b_subln'][l], SUBLN_EPS) * (1.0 - lam_init)).reshape(nb, T, W_B)

    cqkv, cz, cbeta, calpha = _split(pc, [3 * W_C, W_C, H_C, H_C])
    xcat = jnp.concatenate([c_conv.astype(dt), cqkv], axis=1)
    cw = p['c_conv_w'][l]
    conv = xcat[:, 0:T] * cw[0]
    for j in range(1, CONV_W):
        conv = conv + xcat[:, j:j + T] * cw[j]
    new_c_conv = xcat[:, -(CONV_W - 1):]
    cq, ck, cv = (t.reshape(nb, T, H_C, HD_C).astype(f32) for t in _split(jax.nn.silu(conv), [W_C] * 3))
    cq = l2norm(cq) * HD_C ** -0.5
    ck = l2norm(ck)
    beta = jax.nn.sigmoid(cbeta.astype(f32))
    g_log = -jnp.exp(p['c_a_log'][l].astype(f32)) * jax.nn.softplus((calpha + p['c_dt_bias'][l]).astype(f32))
    yc, new_c_delta = gated_delta_chunked(cq, ck, cv, g_log, beta, c_delta.astype(f32))
    yc = rmsnorm(yc, p['c_norm_w'][l]) * jax.nn.silu(cz.reshape(nb, T, H_C, HD_C).astype(f32))
    yc = yc.reshape(nb, T, W_C).astype(dt)

    ga, gb, gc = _split(jax.nn.sigmoid(pg), [D_MODEL] * 3)
    merged = ga * (ya @ p['w_br_a'][l]) + gb * (yb @ p['w_br_b'][l]) + gc * (yc @ p['w_br_c'][l])
    out = merged @ p['w_out'][l]
    return out, (new_a_shift, new_a_wkv, new_c_conv, new_c_delta, bk, bv)


def memory_kv(p, l, mem):
    nb, nm, _ = mem.shape
    mk, mv = _split(rmsnorm(mem, p['g_mem'][l]) @ p['w_mem_kv'][l], [W_MEM, W_MEM])
    return mk.reshape(nb, nm, H_MEM, HD_MEM), mv.reshape(nb, nm, H_MEM, HD_MEM)


def memory_attention(p, l, h, mem_k, mem_v):
    nb, T, _ = h.shape
    q = (h @ p['w_mem_q'][l]).reshape(nb, T, H_MEM, HD_MEM)
    s = jnp.einsum('bqhd,bkhd->bhqk', q, mem_k.astype(h.dtype)).astype(jnp.float32) * HD_MEM ** -0.5
    a = jax.nn.softmax(s, axis=-1).astype(h.dtype)
    o = jnp.einsum('bhqk,bkhd->bqhd', a, mem_v.astype(h.dtype)).reshape(nb, T, W_MEM)
    return o @ p['w_mem_o'][l]


def trunk_layer(p, l, x, mem_k, mem_v, a_shift, a_wkv, c_conv, c_delta, b_k_past, b_v_past):
    y, st = parallel_mixer(p, l, rmsnorm(x, p['g_pre_mix'][l]), a_shift, a_wkv, c_conv, c_delta, b_k_past, b_v_past)
    x = x + rmsnorm(y, p['g_post_mix'][l])
    x = x + rmsnorm(memory_attention(p, l, rmsnorm(x, p['g_pre_mem'][l]), mem_k, mem_v), p['g_post_mem'][l])
    hf = rmsnorm(x, p['g_pre_ffn'][l]) @ p['w_ffn1'][l]
    x = x + rmsnorm(jnp.square(jax.nn.relu(hf)) @ p['w_ffn2'][l], p['g_post_ffn'][l])
    return x, st


def setup_inputs(seed: int = 0) -> dict:
    key = jax.random.key(seed)
    keys = jax.random.split(key, 64)
    ctr = [0]

    def nk():
        ctr[0] += 1
        return keys[ctr[0] - 1]

    f32 = jnp.float32

    def normal(shape, scale):
        return jax.random.normal(nk(), shape, f32) * scale

    def gain(shape):
        return 1.0 + normal(shape, 0.02)

    n_pages = PAST_LEN // PAGE_SIZE
    n_used = DEC_BATCH * n_pages
    n_pool = n_used + max(1, n_used // 4)

    x_prompt = normal((BATCH, SEQ, D_MODEL), 1.0)
    x_sample = normal((DEC_BATCH, DEC_SEQ, D_MODEL), 1.0)
    mem_prompt = normal((BATCH, N_MEM, D_MODEL), 1.0)
    cache_b_k = normal((n_pool, DEPTH, PAGE_SIZE, H_B, 2 * HD_B), 1.0)
    cache_b_v = normal((n_pool, DEPTH, PAGE_SIZE, H_B, 2 * HD_B), 1.0)
    page_table = jax.random.permutation(nk(), n_pool)[:n_used].reshape(DEC_BATCH, n_pages).astype(jnp.int32)
    state_a_wkv = normal((DEPTH, DEC_BATCH, H_A, HD_A, HD_A), 0.1)
    state_a_shift = normal((DEPTH, DEC_BATCH, A_COLS), 1.0)
    state_c_delta = normal((DEPTH, DEC_BATCH, H_C, HD_C, HD_C), 0.1)
    state_c_conv = normal((DEPTH, DEC_BATCH, CONV_W - 1, 3 * W_C), 1.0)
    cache_mem_k = normal((DEPTH, DEC_BATCH, N_MEM, H_MEM, HD_MEM), 1.0)
    cache_mem_v = normal((DEPTH, DEC_BATCH, N_MEM, H_MEM, HD_MEM), 1.0)

    g_pre_mix = gain((DEPTH, D_MODEL))
    g_post_mix = gain((DEPTH, D_MODEL))
    g_pre_mem = gain((DEPTH, D_MODEL))
    g_post_mem = gain((DEPTH, D_MODEL))
    g_pre_ffn = gain((DEPTH, D_MODEL))
    g_post_ffn = gain((DEPTH, D_MODEL))
    w_in = normal((DEPTH, D_MODEL, IN_COLS), D_MODEL ** -0.5)

    a_mu = jax.random.uniform(nk(), (DEPTH, A_COLS), f32)
    a_w0 = jax.random.uniform(nk(), (DEPTH, W_A), f32, -6.0, 0.0)
    a_w2 = normal((DEPTH, LORA_W, W_A), 0.1)
    a_a0 = normal((DEPTH, W_A), 0.1)
    a_a2 = normal((DEPTH, LORA_A, W_A), LORA_A ** -0.5)
    a_g2 = normal((DEPTH, LORA_G, W_A), LORA_G ** -0.5)
    a_kk = 0.85 + normal((DEPTH, W_A), 0.02)
    a_ka = 1.0 + normal((DEPTH, W_A), 0.02)
    a_rk = normal((DEPTH, H_A, HD_A), 0.1)
    a_ln_w = gain((DEPTH, W_A))
    a_ln_b = normal((DEPTH, W_A), 0.02)

    b_lam_q1 = normal((DEPTH, HD_B), 0.1)
    b_lam_k1 = normal((DEPTH, HD_B), 0.1)
    b_lam_q2 = normal((DEPTH, HD_B), 0.1)
    b_lam_k2 = normal((DEPTH, HD_B), 0.1)
    b_subln = gain((DEPTH, 2 * HD_B))
    rel_bias = normal((N_BUCKETS, H_B), 0.5)

    c_conv_w = normal((DEPTH, CONV_W, 3 * W_C), CONV_W ** -0.5)
    c_a_log = jnp.log(jax.random.uniform(nk(), (DEPTH, H_C), f32, 1.0, 16.0))
    dt0 = jax.random.uniform(nk(), (DEPTH, H_C), f32, 1e-3, 1e-1)
    c_dt_bias = jnp.log(jnp.expm1(dt0))
    c_norm_w = gain((DEPTH, HD_C))

    w_br_a = normal((DEPTH, W_A, D_MODEL), W_A ** -0.5)
    w_br_b = normal((DEPTH, W_B, D_MODEL), W_B ** -0.5)
    w_br_c = normal((DEPTH, W_C, D_MODEL), W_C ** -0.5)
    w_out = normal((DEPTH, D_MODEL, D_MODEL), D_MODEL ** -0.5)

    g_mem = gain((DEPTH, D_MODEL))
    w_mem_q = normal((DEPTH, D_MODEL, W_MEM), D_MODEL ** -0.5)
    w_mem_kv = normal((DEPTH, D_MODEL, 2 * W_MEM), D_MODEL ** -0.5)
    w_mem_o = normal((DEPTH, W_MEM, D_MODEL), W_MEM ** -0.5)

    w_ffn1 = normal((DEPTH, D_MODEL, D_FF), D_MODEL ** -0.5)
    w_ffn2 = normal((DEPTH, D_FF, D_MODEL), D_FF ** -0.5)

    return {'x_prompt': x_prompt, 'x_sample': x_sample, 'mem_prompt': mem_prompt,
            'cache_b_k': cache_b_k, 'cache_b_v': cache_b_v, 'page_table': page_table,
            'state_a_wkv': state_a_wkv, 'state_a_shift': state_a_shift,
            'state_c_delta': state_c_delta, 'state_c_conv': state_c_conv,
            'cache_mem_k': cache_mem_k, 'cache_mem_v': cache_mem_v,
            'g_pre_mix': g_pre_mix, 'g_post_mix': g_post_mix, 'g_pre_mem': g_pre_mem,
            'g_post_mem': g_post_mem, 'g_pre_ffn': g_pre_ffn, 'g_post_ffn': g_post_ffn,
            'w_in': w_in,
            'a_mu': a_mu, 'a_w0': a_w0, 'a_w2': a_w2, 'a_a0': a_a0, 'a_a2': a_a2, 'a_g2': a_g2,
            'a_kk': a_kk, 'a_ka': a_ka, 'a_rk': a_rk, 'a_ln_w': a_ln_w, 'a_ln_b': a_ln_b,
            'b_lam_q1': b_lam_q1, 'b_lam_k1': b_lam_k1, 'b_lam_q2': b_lam_q2, 'b_lam_k2': b_lam_k2,
            'b_subln': b_subln, 'rel_bias': rel_bias,
            'c_conv_w': c_conv_w, 'c_a_log': c_a_log, 'c_dt_bias': c_dt_bias, 'c_norm_w': c_norm_w,
            'w_br_a': w_br_a, 'w_br_b': w_br_b, 'w_br_c': w_br_c, 'w_out': w_out,
            'g_mem': g_mem, 'w_mem_q': w_mem_q, 'w_mem_kv': w_mem_kv, 'w_mem_o': w_mem_o,
            'w_ffn1': w_ffn1, 'w_ffn2': w_ffn2}


def reference(x_prompt, x_sample, mem_prompt, cache_b_k, cache_b_v, page_table,
              state_a_wkv, state_a_shift, state_c_delta, state_c_conv, cache_mem_k, cache_mem_v,
              g_pre_mix, g_post_mix, g_pre_mem, g_post_mem, g_pre_ffn, g_post_ffn,
              w_in,
              a_mu, a_w0, a_w2, a_a0, a_a2, a_g2, a_kk, a_ka, a_rk, a_ln_w, a_ln_b,
              b_lam_q1, b_lam_k1, b_lam_q2, b_lam_k2, b_subln, rel_bias,
              c_conv_w, c_a_log, c_dt_bias, c_norm_w,
              w_br_a, w_br_b, w_br_c, w_out,
              g_mem, w_mem_q, w_mem_kv, w_mem_o,
              w_ffn1, w_ffn2):
    p = dict(g_pre_mix=g_pre_mix, g_post_mix=g_post_mix, g_pre_mem=g_pre_mem, g_post_mem=g_post_mem,
             g_pre_ffn=g_pre_ffn, g_post_ffn=g_post_ffn, w_in=w_in,
             a_mu=a_mu, a_w0=a_w0, a_w2=a_w2, a_a0=a_a0, a_a2=a_a2, a_g2=a_g2, a_kk=a_kk, a_ka=a_ka,
             a_rk=a_rk, a_ln_w=a_ln_w, a_ln_b=a_ln_b,
             b_lam_q1=b_lam_q1, b_lam_k1=b_lam_k1, b_lam_q2=b_lam_q2, b_lam_k2=b_lam_k2,
             b_subln=b_subln, rel_bias=rel_bias,
             c_conv_w=c_conv_w, c_a_log=c_a_log, c_dt_bias=c_dt_bias, c_norm_w=c_norm_w,
             w_br_a=w_br_a, w_br_b=w_br_b, w_br_c=w_br_c, w_out=w_out,
             g_mem=g_mem, w_mem_q=w_mem_q, w_mem_kv=w_mem_kv, w_mem_o=w_mem_o,
             w_ffn1=w_ffn1, w_ffn2=w_ffn2)
    nbp = x_prompt.shape[0]
    nbs = x_sample.shape[0]
    n_pages = page_table.shape[1]
    dtp = x_prompt.dtype
    xp, xs = x_prompt, x_sample
    bkp, bvp, bks, bvs = [], [], [], []
    wkvp, wkvs, shp, shs, dlp, dls, cvp, cvs, mkp, mvp = [], [], [], [], [], [], [], [], [], []
    for l in range(DEPTH):
        mk, mv = memory_kv(p, l, mem_prompt)
        xp, (sh, wkv, cv, dl, bk, bv) = trunk_layer(
            p, l, xp, mk, mv,
            jnp.zeros((nbp, A_COLS), dtp), jnp.zeros((nbp, H_A, HD_A, HD_A), jnp.float32),
            jnp.zeros((nbp, CONV_W - 1, 3 * W_C), dtp), jnp.zeros((nbp, H_C, HD_C, HD_C), jnp.float32),
            None, None)
        shp.append(sh); wkvp.append(wkv); cvp.append(cv); dlp.append(dl)
        bkp.append(bk); bvp.append(bv); mkp.append(mk); mvp.append(mv)
        k_past = cache_b_k[page_table, l].reshape(nbs, n_pages * PAGE_SIZE, H_B, 2 * HD_B)
        v_past = cache_b_v[page_table, l].reshape(nbs, n_pages * PAGE_SIZE, H_B, 2 * HD_B)
        xs, (sh, wkv, cv, dl, bk, bv) = trunk_layer(
            p, l, xs, cache_mem_k[l], cache_mem_v[l],
            state_a_shift[l], state_a_wkv[l], state_c_conv[l], state_c_delta[l], k_past, v_past)
        shs.append(sh); wkvs.append(wkv); cvs.append(cv); dls.append(dl)
        bks.append(bk); bvs.append(bv)
    y_prompt = xp
    y_sample = xs
    new_b_k_prompt = jnp.stack(bkp, axis=1)
    new_b_v_prompt = jnp.stack(bvp, axis=1)
    new_b_k_sample = jnp.stack(bks, axis=1)
    new_b_v_sample = jnp.stack(bvs, axis=1)
    new_a_wkv_prompt = jnp.stack(wkvp, axis=0)
    new_a_wkv_sample = jnp.stack(wkvs, axis=0)
    new_a_shift_prompt = jnp.stack(shp, axis=0)
    new_a_shift_sample = jnp.stack(shs, axis=0)
    new_c_delta_prompt = jnp.stack(dlp, axis=0)
    new_c_delta_sample = jnp.stack(dls, axis=0)
    new_c_conv_prompt = jnp.stack(cvp, axis=0)
    new_c_conv_sample = jnp.stack(cvs, axis=0)
    new_mem_k_prompt = jnp.stack(mkp, axis=0)
    new_mem_v_prompt = jnp.stack(mvp, axis=0)
    return (y_prompt, y_sample, new_b_k_prompt, new_b_v_prompt, new_b_k_sample, new_b_v_sample,
            new_a_wkv_prompt, new_a_wkv_sample, new_a_shift_prompt, new_a_shift_sample,
            new_c_delta_prompt, new_c_delta_sample, new_c_conv_prompt, new_c_conv_sample,
            new_mem_k_prompt, new_mem_v_prompt)
```

```python
import functools
import math

import jax
import jax.numpy as jnp
from jax import lax
from jax.experimental import pallas as pl
from jax.experimental.pallas import tpu as pltpu

F32 = jnp.float32
BF16 = jnp.bfloat16

RMS_EPS = 1e-6
GN_EPS_A = 64e-5
SUBLN_EPS = 1e-5
N_BUCKETS = 32
MAX_DISTANCE = 128
CONV_W = 4
LORA_W = 64
LORA_A = 64
LORA_G = 128

LANES = 128
SUBLANES = 8
VMEM_LIMIT_BYTES = 56 * 1024 * 1024

CHUNK = 64
ATT_BLOCK = 128
NEG_INF = -1e30


def _params(*sem):
    return pltpu.CompilerParams(dimension_semantics=sem, vmem_limit_bytes=VMEM_LIMIT_BYTES)


def _tile(n, pref):
    if n <= pref:
        return n
    for t in range(pref, 7, -1):
        if n % t == 0 and t % SUBLANES == 0:
            return t
    return n


def _dot(a, b):
    return jnp.dot(a.astype(BF16), b.astype(BF16), preferred_element_type=F32)


def _dot_nt(a, b):
    return lax.dot_general(a.astype(BF16), b.astype(BF16), (((1,), (1,)), ((), ())), preferred_element_type=F32)


def _dot_tn(a, b):
    return lax.dot_general(a.astype(BF16), b.astype(BF16), (((0,), (0,)), ((), ())), preferred_element_type=F32)


def _dot_f32(a, b):
    return jnp.dot(a, b, preferred_element_type=F32, precision=lax.Precision.HIGHEST)


def _rms(x, g, eps):
    return x * lax.rsqrt(jnp.mean(x * x, axis=-1, keepdims=True) + eps) * g


def _tri_masks(n):
    r = lax.broadcasted_iota(jnp.int32, (n, n), 0)
    c = lax.broadcasted_iota(jnp.int32, (n, n), 1)
    return r >= c, r > c


def _block_pair_masks(n):
    r = lax.broadcasted_iota(jnp.int32, (n, n), 0)
    c = lax.broadcasted_iota(jnp.int32, (n, n), 1)
    masks = []
    s = 1
    while s < n:
        masks.append(((r // (2 * s)) == (c // (2 * s))) & ((r // s) % 2 == 1) & ((c // s) % 2 == 0))
        s *= 2
    return masks


def _unit_lower_inverse_minus_eye(nmat, masks):
    q = jnp.where(masks[0], nmat, 0.0)
    for mask in masks[1:]:
        n_s = jnp.where(mask, nmat, 0.0)
        y = n_s + _dot(q, n_s)
        q = q + y + _dot(y, q)
    return q


def _mm_kernel(x_ref, g_ref, w_ref, o_ref, xn_ref, *, norm, act):
    @pl.when(pl.program_id(1) == 0)
    def _():
        x = x_ref[...].astype(F32)
        if norm:
            x = _rms(x, g_ref[...], RMS_EPS)
        xn_ref[...] = x.astype(BF16)

    y = jnp.dot(xn_ref[...], w_ref[...], preferred_element_type=F32)
    if act == "relu2":
        y = jnp.square(jnp.maximum(y, 0.0))
    o_ref[...] = y.astype(o_ref.dtype)


def _mm(x, w, g=None, act=None, out_dtype=F32, tm=512, tn=512):
    m, k = x.shape
    n = w.shape[1]
    tm = _tile(m, tm)
    tn = _tile(n, tn)
    norm = g is not None
    if g is None:
        g = jnp.ones((1, k), F32)
    return pl.pallas_call(
        functools.partial(_mm_kernel, norm=norm, act=act),
        out_shape=jax.ShapeDtypeStruct((m, n), out_dtype),
        grid=(m // tm, n // tn),
        in_specs=[pl.BlockSpec((tm, k), lambda i, j: (i, 0)),
                  pl.BlockSpec((1, k), lambda i, j: (0, 0)),
                  pl.BlockSpec((k, tn), lambda i, j: (0, j))],
        out_specs=pl.BlockSpec((tm, tn), lambda i, j: (i, j)),
        scratch_shapes=[pltpu.VMEM((tm, k), BF16)],
        compiler_params=_params("parallel", "arbitrary"),
        name="mm_norm",
    )(x, g, w)


def _mm_post_kernel(y_ref, w_ref, g_ref, res_ref, o_ref, acc_ref):
    kk = pl.program_id(1)

    @pl.when(kk == 0)
    def _():
        acc_ref[...] = jnp.zeros_like(acc_ref)

    acc_ref[...] += jnp.dot(y_ref[...].astype(BF16), w_ref[...], preferred_element_type=F32)

    @pl.when(kk == pl.num_programs(1) - 1)
    def _():
        o_ref[...] = res_ref[...] + _rms(acc_ref[...], g_ref[...], RMS_EPS)


def _mm_post(y, w, g, res, tm=256, tk=1024):
    m, k = y.shape
    n = w.shape[1]
    tm = _tile(m, tm)
    tk = _tile(k, tk)
    return pl.pallas_call(
        _mm_post_kernel,
        out_shape=jax.ShapeDtypeStruct((m, n), F32),
        grid=(m // tm, k // tk),
        in_specs=[pl.BlockSpec((tm, tk), lambda i, j: (i, j)),
                  pl.BlockSpec((tk, n), lambda i, j: (j, 0)),
                  pl.BlockSpec((1, n), lambda i, j: (0, 0)),
                  pl.BlockSpec((tm, n), lambda i, j: (i, 0))],
        out_specs=pl.BlockSpec((tm, n), lambda i, j: (i, 0)),
        scratch_shapes=[pltpu.VMEM((tm, n), F32)],
        compiler_params=_params("parallel", "arbitrary"),
        name="mm_post",
    )(y, w, g, res)


def _merge_kernel(ya_ref, yb_ref, yc_ref, pg_ref, x_ref, wa_ref, wb_ref, wc_ref, wo_ref, g_ref, o_ref):
    d = x_ref.shape[1]
    gates = jax.nn.sigmoid(pg_ref[...])
    merged = (gates[:, 0:d] * jnp.dot(ya_ref[...], wa_ref[...], preferred_element_type=F32)
              + gates[:, d:2 * d] * jnp.dot(yb_ref[...], wb_ref[...], preferred_element_type=F32)
              + gates[:, 2 * d:3 * d] * jnp.dot(yc_ref[...], wc_ref[...], preferred_element_type=F32))
    out = jnp.dot(merged.astype(BF16), wo_ref[...], preferred_element_type=F32)
    o_ref[...] = x_ref[...] + _rms(out, g_ref[...], RMS_EPS)


def _merge(ya, yb, yc, pg, x, wa, wb, wc, wo, g, tm=256):
    m, d = x.shape
    tm = _tile(m, tm)
    row = lambda i: (i, 0)
    full = lambda i: (0, 0)
    return pl.pallas_call(
        _merge_kernel,
        out_shape=jax.ShapeDtypeStruct((m, d), F32),
        grid=(m // tm,),
        in_specs=[pl.BlockSpec((tm, ya.shape[1]), row), pl.BlockSpec((tm, yb.shape[1]), row),
                  pl.BlockSpec((tm, yc.shape[1]), row), pl.BlockSpec((tm, 3 * d), row),
                  pl.BlockSpec((tm, d), row),
                  pl.BlockSpec(wa.shape, full), pl.BlockSpec(wb.shape, full), pl.BlockSpec(wc.shape, full),
                  pl.BlockSpec(wo.shape, full), pl.BlockSpec((1, d), full)],
        out_specs=pl.BlockSpec((tm, d), row),
        compiler_params=_params("parallel"),
        name="merge",
    )(ya, yb, yc, pg, x, wa, wb, wc, wo, g)


def _rwkv_kernel(pa_ref, sh_ref, s0_ref, mu_ref, w0_ref, w2_ref, a0_ref, a2_ref, g2_ref, kk_ref, ka_ref, rk_ref,
                 lnw_ref, lnb_ref, y_ref, sout_ref, state_ref, carry_ref, *, chunk, t_valid, n_heads, hd):
    c = pl.program_id(1)
    width = n_heads * hd

    @pl.when(c == 0)
    def _():
        state_ref[...] = s0_ref[0]
        carry_ref[...] = sh_ref[0]

    pa = pa_ref[0]
    row = lax.broadcasted_iota(jnp.int32, (chunk, 1), 0)
    prev = jnp.where(row == 0, carry_ref[...], pltpu.roll(pa, 1, axis=0))
    carry_ref[...] = pa[chunk - 1:chunk, :]
    xa = pa + (prev - pa) * mu_ref[...]
    ar = xa[:, 0:width]
    ak = xa[:, width:2 * width]
    av = xa[:, 2 * width:3 * width]
    o = 3 * width
    aw = xa[:, o:o + LORA_W]
    aa = xa[:, o + LORA_W:o + LORA_W + LORA_A]
    ag = xa[:, o + LORA_W + LORA_A:o + LORA_W + LORA_A + LORA_G]

    w_raw = w0_ref[...] + _dot(jnp.tanh(aw), w2_ref[...])
    log_w = -jnp.exp(-jax.nn.softplus(-w_raw) - 0.5)
    a_in = jax.nn.sigmoid(a0_ref[...] + _dot(aa, a2_ref[...]))
    g_out = _dot(jax.nn.sigmoid(ag), g2_ref[...])
    kk_all = ak * kk_ref[...]
    k_all = ak * (1.0 + (a_in - 1.0) * ka_ref[...])
    rk_all = ar * k_all * rk_ref[...]

    valid = (row + c * chunk) < t_valid
    tri, strict = _tri_masks(chunk)
    pair_masks = _block_pair_masks(chunk)
    tri_f = tri.astype(F32)
    ln_w = lnw_ref[...]
    ln_b = lnb_ref[...]

    outs = []
    for h in range(n_heads):
        sl = slice(h * hd, (h + 1) * hd)
        r = ar[:, sl]
        v = av[:, sl]
        kk = kk_all[:, sl]
        kk = kk * lax.rsqrt(jnp.sum(kk * kk, axis=-1, keepdims=True) + 1e-12)
        k = jnp.where(valid, k_all[:, sl], 0.0)
        a_vec = -kk
        b_vec = jnp.where(valid, kk * a_in[:, sl], 0.0)
        lw = jnp.where(valid, log_w[:, sl], 0.0)

        g_inc = _dot_f32(tri_f, lw)
        g_exc = g_inc - lw
        g_last = g_inc[chunk - 1:chunk, :]
        e_neg = jnp.exp(-g_inc)
        x_mat = jnp.concatenate([a_vec * jnp.exp(g_exc), r * jnp.exp(g_inc)], axis=0)
        y_mat = jnp.concatenate([b_vec * e_neg, k * e_neg], axis=0)
        mm = _dot_nt(x_mat, y_mat)
        a_ab = jnp.where(strict, mm[0:chunk, 0:chunk], 0.0)
        a_ak = jnp.where(strict, mm[0:chunk, chunk:2 * chunk], 0.0)
        r_b = jnp.where(tri, mm[chunk:2 * chunk, 0:chunk], 0.0)
        r_k = jnp.where(tri, mm[chunk:2 * chunk, chunk:2 * chunk], 0.0)
        t_q = _unit_lower_inverse_minus_eye(a_ab, pair_masks)

        s0 = state_ref[h]
        xs = _dot_nt(x_mat, s0)
        rhs = xs[0:chunk] + _dot(a_ak, v)
        u = rhs + _dot(t_q, rhs)
        y = xs[chunk:2 * chunk] + _dot(r_b, u) + _dot(r_k, v)
        e_tail = jnp.exp(g_last - g_inc)
        s_new = (s0 * jnp.exp(g_last)
                 + _dot_tn(jnp.concatenate([u, v], axis=0),
                           jnp.concatenate([b_vec * e_tail, k * e_tail], axis=0)))
        state_ref[h] = s_new

        mean = jnp.mean(y, axis=-1, keepdims=True)
        var = jnp.mean(jnp.square(y - mean), axis=-1, keepdims=True)
        yn = (y - mean) * lax.rsqrt(var + GN_EPS_A) * ln_w[:, sl] + ln_b[:, sl]
        bonus = jnp.sum(rk_all[:, sl], axis=-1, keepdims=True) * v
        outs.append((yn + bonus) * g_out[:, sl])

    y_ref[0] = jnp.concatenate(outs, axis=-1).astype(y_ref.dtype)

    @pl.when(c == pl.num_programs(1) - 1)
    def _():
        sout_ref[0] = state_ref[...]


def _rwkv(pa3, a_shift, wkv0, lp, chunk):
    nb, t, cols = pa3.shape
    n_heads, hd = wkv0.shape[1], wkv0.shape[2]
    width = n_heads * hd
    tp = -(-t // chunk) * chunk
    if tp != t:
        pa3 = jnp.pad(pa3, ((0, 0), (0, tp - t), (0, 0)))
    full2 = lambda b, c: (0, 0)
    vec = lambda n: pl.BlockSpec((1, n), full2)
    y, s_new = pl.pallas_call(
        functools.partial(_rwkv_kernel, chunk=chunk, t_valid=t, n_heads=n_heads, hd=hd),
        out_shape=(jax.ShapeDtypeStruct((nb, tp, width), BF16),
                   jax.ShapeDtypeStruct(wkv0.shape, F32)),
        grid=(nb, tp // chunk),
        in_specs=[pl.BlockSpec((1, chunk, cols), lambda b, c: (b, c, 0)),
                  pl.BlockSpec((1, 1, cols), lambda b, c: (b, 0, 0)),
                  pl.BlockSpec((1, n_heads, hd, hd), lambda b, c: (b, 0, 0, 0)),
                  vec(cols), vec(width), pl.BlockSpec((LORA_W, width), full2),
                  vec(width), pl.BlockSpec((LORA_A, width), full2), pl.BlockSpec((LORA_G, width), full2),
                  vec(width), vec(width), vec(width), vec(width), vec(width)],
        out_specs=(pl.BlockSpec((1, chunk, width), lambda b, c: (b, c, 0)),
                   pl.BlockSpec((1, n_heads, hd, hd), lambda b, c: (b, 0, 0, 0))),
        scratch_shapes=[pltpu.VMEM((n_heads, hd, hd), F32), pltpu.VMEM((1, cols), F32)],
        compiler_params=_params("parallel", "arbitrary"),
        name="rwkv7",
    )(pa3, a_shift.reshape(nb, 1, cols), wkv0, lp["a_mu"], lp["a_w0"], lp["a_w2"], lp["a_a0"], lp["a_a2"],
      lp["a_g2"], lp["a_kk"], lp["a_ka"], lp["a_rk"], lp["a_ln_w"], lp["a_ln_b"])
    return y[:, :t], s_new


def _gdn_kernel(pc_ref, ps_ref, cv_ref, s0_ref, cw_ref, alog_ref, dtb_ref, nw_ref,
                y_ref, sout_ref, state_ref, carry_ref, *, chunk, t_valid, n_heads, hd):
    c = pl.program_id(1)
    width = n_heads * hd
    taps = CONV_W - 1

    @pl.when(c == 0)
    def _():
        state_ref[...] = s0_ref[0]
        carry_ref[...] = cv_ref[0]

    pc = pc_ref[0]
    x = pc[:, 0:3 * width]
    cz = pc[:, 3 * width:4 * width]
    row = lax.broadcasted_iota(jnp.int32, (chunk, 1), 0)
    conv = x * cw_ref[taps:taps + 1, :]
    for j in range(1, CONV_W):
        shifted = pltpu.roll(x, j, axis=0)
        for i in range(j):
            shifted = jnp.where(row == i, carry_ref[taps - j + i:taps - j + i + 1, :], shifted)
        conv = conv + shifted * cw_ref[taps - j:taps - j + 1, :]
    carry_ref[...] = x[chunk - taps:chunk, :]
    act = jax.nn.silu(conv)
    q_all = act[:, 0:width]
    k_all = act[:, width:2 * width]
    v_all = act[:, 2 * width:3 * width]

    valid = (row + c * chunk) < t_valid
    ps = ps_ref[0]
    beta_all = jnp.where(valid, jax.nn.sigmoid(ps), 0.0)
    lane = lax.broadcasted_iota(jnp.int32, (1, ps.shape[1]), 1)
    dt_full = dtb_ref[...]
    neg_rate = -jnp.exp(alog_ref[...])
    g_all = jnp.where(valid & (lane >= n_heads) & (lane < 2 * n_heads),
                      neg_rate * jax.nn.softplus(ps + dt_full), 0.0)
    tri, strict = _tri_masks(chunk)
    pair_masks = _block_pair_masks(chunk)
    gc_all = _dot_f32(tri.astype(F32), g_all)
    gc_rows = gc_all.T

    outs = []
    for h in range(n_heads):
        sl = slice(h * hd, (h + 1) * hd)
        q = q_all[:, sl]
        q = q * lax.rsqrt(jnp.sum(q * q, axis=-1, keepdims=True) + 1e-6) * (hd ** -0.5)
        k = k_all[:, sl]
        k = k * lax.rsqrt(jnp.sum(k * k, axis=-1, keepdims=True) + 1e-6)
        v = v_all[:, sl]
        beta = beta_all[:, h:h + 1]
        gcol = gc_all[:, n_heads + h:n_heads + h + 1]
        grow = gc_rows[n_heads + h:n_heads + h + 1, :]
        decay = jnp.where(tri, jnp.exp(jnp.where(tri, gcol - grow, 0.0)), 0.0)
        e_g = jnp.exp(gcol)
        kb = k * beta
        mm = _dot_nt(jnp.concatenate([kb, q], axis=0), k)
        lmat = jnp.where(strict, mm[0:chunk] * decay, 0.0)
        attn = mm[chunk:2 * chunk] * decay
        t_q = _unit_lower_inverse_minus_eye(-lmat, pair_masks)
        rhs = jnp.concatenate([v * beta, kb * e_g], axis=1)
        uw = rhs + _dot(t_q, rhs)
        u = uw[:, 0:hd]
        wk = uw[:, hd:2 * hd]

        s0 = state_ref[h]
        ws = _dot(jnp.concatenate([wk, q * e_g], axis=0), s0)
        v_new = u - ws[0:chunk]
        o = ws[chunk:2 * chunk] + _dot(attn, v_new)
        g_last = gcol[chunk - 1:chunk, :]
        state_ref[h] = s0 * jnp.exp(g_last) + _dot_tn(k * jnp.exp(g_last - gcol), v_new)

        on = o * lax.rsqrt(jnp.mean(o * o, axis=-1, keepdims=True) + RMS_EPS) * nw_ref[...]
        outs.append(on * jax.nn.silu(cz[:, sl]))

    y_ref[0] = jnp.concatenate(outs, axis=-1).astype(y_ref.dtype)

    @pl.when(c == pl.num_programs(1) - 1)
    def _():
        sout_ref[0] = state_ref[...]


def _gdn(pc3, ps3, c_conv, delta0, lp, chunk):
    nb, t, cols = pc3.shape
    n_heads, hd = delta0.shape[1], delta0.shape[2]
    width = n_heads * hd
    taps = CONV_W - 1
    tp = -(-t // chunk) * chunk
    if tp != t:
        pc3 = jnp.pad(pc3, ((0, 0), (0, tp - t), (0, 0)))
        ps3 = jnp.pad(ps3, ((0, 0), (0, tp - t), (0, 0)))
    full2 = lambda b, c: (0, 0)
    y, s_new = pl.pallas_call(
        functools.partial(_gdn_kernel, chunk=chunk, t_valid=t, n_heads=n_heads, hd=hd),
        out_shape=(jax.ShapeDtypeStruct((nb, tp, width), BF16),
                   jax.ShapeDtypeStruct(delta0.shape, F32)),
        grid=(nb, tp // chunk),
        in_specs=[pl.BlockSpec((1, chunk, cols), lambda b, c: (b, c, 0)),
                  pl.BlockSpec((1, chunk, LANES), lambda b, c: (b, c, 0)),
                  pl.BlockSpec((1, taps, 3 * width), lambda b, c: (b, 0, 0)),
                  pl.BlockSpec((1, n_heads, hd, hd), lambda b, c: (b, 0, 0, 0)),
                  pl.BlockSpec((CONV_W, 3 * width), full2),
                  pl.BlockSpec((1, LANES), full2), pl.BlockSpec((1, LANES), full2),
                  pl.BlockSpec((1, hd), full2)],
        out_specs=(pl.BlockSpec((1, chunk, width), lambda b, c: (b, c, 0)),
                   pl.BlockSpec((1, n_heads, hd, hd), lambda b, c: (b, 0, 0, 0))),
        scratch_shapes=[pltpu.VMEM((n_heads, hd, hd), F32), pltpu.VMEM((taps, 3 * width), F32)],
        compiler_params=_params("parallel", "arbitrary"),
        name="gdn",
    )(pc3, ps3, c_conv, delta0, lp["c_conv_w"], lp["c_a_log_pad"], lp["c_dt_bias_pad"], lp["c_norm_w"])
    return y[:, :t], s_new


def _t5_bucket(dist):
    n = jnp.maximum(dist, 0)
    max_exact = N_BUCKETS // 2
    nf = jnp.maximum(n, 1).astype(F32)
    large = max_exact + (jnp.log(nf / max_exact) / math.log(MAX_DISTANCE / max_exact)
                         * (N_BUCKETS - max_exact)).astype(jnp.int32)
    return jnp.where(n < max_exact, n, jnp.minimum(large, N_BUCKETS - 1))


def _diff_attn_kernel(rb_ref, sc_ref, q_ref, k_ref, v_ref, sub_ref, o_ref, bias_ref, *, blk, hd):
    h = pl.program_id(1)
    qi = pl.program_id(2)
    far_bias = rb_ref[N_BUCKETS - 1, h]

    @pl.when(qi == 0)
    def _():
        r = lax.broadcasted_iota(jnp.int32, (blk, blk), 0)
        c = lax.broadcasted_iota(jnp.int32, (blk, blk), 1)
        for delta in range(2):
            dist = delta * blk + r - c
            bucket = _t5_bucket(dist)
            bias = jnp.zeros((blk, blk), F32)
            for n in range(N_BUCKETS):
                bias = jnp.where(bucket == n, rb_ref[n, h], bias)
            bias_ref[delta] = jnp.where(dist >= 0, bias, NEG_INF)

    q = q_ref[0] * (hd ** -0.5)
    qs = (q[:, 0:hd].astype(BF16), q[:, hd:2 * hd].astype(BF16))

    def step(kj, bias, carry):
        start = pl.multiple_of(kj * blk, blk)
        kt = k_ref[0, pl.ds(start, blk), :]
        vt = v_ref[0, pl.ds(start, blk), :].astype(BF16)
        new = []
        for comp in range(2):
            m, l, acc = carry[comp]
            s = _dot_nt(qs[comp], kt[:, comp * hd:(comp + 1) * hd]) + bias
            m_new = jnp.maximum(m, jnp.max(s, axis=-1, keepdims=True))
            alpha = jnp.exp(m - m_new)
            p = jnp.exp(s - m_new)
            l_new = alpha * l + jnp.sum(p, axis=-1, keepdims=True)
            acc_new = alpha * acc + jnp.dot(p.astype(BF16), vt, preferred_element_type=F32)
            new.append((m_new, l_new, acc_new))
        return tuple(new)

    init = tuple((jnp.full((blk, 1), NEG_INF, F32), jnp.zeros((blk, 1), F32), jnp.zeros((blk, 2 * hd), F32))
                 for _ in range(2))
    carry = lax.fori_loop(0, jnp.maximum(qi - 1, 0), lambda kj, cr: step(kj, far_bias, cr), init)
    carry = lax.cond(qi >= 1, lambda cr: step(qi - 1, bias_ref[1], cr), lambda cr: cr, carry)
    carry = step(qi, bias_ref[0], carry)

    (_, l1, acc1), (_, l2, acc2) = carry
    out = acc1 / l1 - sc_ref[0] * (acc2 / l2)
    out = _rms(out, sub_ref[...], SUBLN_EPS) * sc_ref[1]
    o_ref[0] = out.astype(o_ref.dtype)


def _diff_attn_prompt(pb3, rel_bias, scal, subln):
    nb, t, cols = pb3.shape
    n_heads = rel_bias.shape[1]
    hw = cols // (3 * n_heads)
    blk = ATT_BLOCK
    assert t % blk == 0 and MAX_DISTANCE <= blk
    smem = pl.BlockSpec(memory_space=pltpu.SMEM)
    return pl.pallas_call(
        functools.partial(_diff_attn_kernel, blk=blk, hd=hw // 2),
        out_shape=jax.ShapeDtypeStruct((nb, t, n_heads * hw), BF16),
        grid=(nb, n_heads, t // blk),
        in_specs=[smem, smem,
                  pl.BlockSpec((1, blk, hw), lambda b, h, i: (b, i, h)),
                  pl.BlockSpec((1, t, hw), lambda b, h, i: (b, 0, n_heads + h)),
                  pl.BlockSpec((1, t, hw), lambda b, h, i: (b, 0, 2 * n_heads + h)),
                  pl.BlockSpec((1, hw), lambda b, h, i: (0, 0))],
        out_specs=pl.BlockSpec((1, blk, hw), lambda b, h, i: (b, i, h)),
        scratch_shapes=[pltpu.VMEM((2, blk, blk), F32)],
        compiler_params=_params("parallel", "parallel", "arbitrary"),
        name="diff_attn_prompt",
    )(rel_bias, scal, pb3, pb3, pb3, subln)


def _paged_attn_kernel(pt_ref, sc_ref, q_ref, kn_ref, vn_ref, k_ref, v_ref, rb_ref, sub_ref, o_ref,
                       m_ref, l_ref, acc_ref, bias_ref, *, n_heads, hd, page):
    p = pl.program_id(1)
    n_pages = pl.num_programs(1)
    n_maps = 2 * n_heads
    width = n_heads * 2 * hd

    lane_grp = lax.broadcasted_iota(jnp.int32, (width, n_maps), 0) // hd
    seg = (lane_grp == lax.broadcasted_iota(jnp.int32, (width, n_maps), 1)).astype(BF16)
    eye = (lax.broadcasted_iota(jnp.int32, (n_maps, n_maps), 0)
           == lax.broadcasted_iota(jnp.int32, (n_maps, n_maps), 1)).astype(F32)

    def to_col(rowvec):
        return jnp.sum(eye * rowvec, axis=1, keepdims=True)

    def bias_rows(dist):
        bucket = _t5_bucket(dist)
        bias = jnp.zeros((dist.shape[0], n_maps), F32)
        for n in range(N_BUCKETS):
            bias = jnp.where(bucket == n, rb_ref[n:n + 1, :], bias)
        return bias

    @pl.when(p == 0)
    def _():
        m_ref[...] = jnp.full(m_ref.shape, NEG_INF, F32)
        l_ref[...] = jnp.zeros_like(l_ref)
        acc_ref[...] = jnp.zeros_like(acc_ref)
        bias_ref[...] = jnp.broadcast_to(rb_ref[N_BUCKETS - 1:N_BUCKETS, :], bias_ref.shape)

    @pl.when(p == n_pages - 1)
    def _():
        r = lax.broadcasted_iota(jnp.int32, (page, 1), 0)
        bias_ref[...] = bias_rows(page - r)

    q = q_ref[0] * (hd ** -0.5)

    def accumulate(keys, vals, bias):
        s = jnp.dot((keys * q).astype(BF16), seg, preferred_element_type=F32) + bias
        m_old = m_ref[...]
        m_new = jnp.maximum(m_old, jnp.max(s, axis=0, keepdims=True))
        alpha = jnp.exp(m_old - m_new)
        pr = jnp.exp(s - m_new)
        l_ref[...] = alpha * l_ref[...] + jnp.sum(pr, axis=0, keepdims=True)
        m_ref[...] = m_new
        pv = to_col(pr) * vals if keys.shape[0] == 1 else _dot_tn(pr, vals)
        acc_ref[...] = acc_ref[...] * to_col(alpha) + pv

    accumulate(k_ref[0, 0], v_ref[0, 0], bias_ref[...])

    @pl.when(p == n_pages - 1)
    def _():
        accumulate(kn_ref[0], vn_ref[0], bias_rows(jnp.zeros((1, 1), jnp.int32)))
        o = acc_ref[...] / to_col(l_ref[...])
        outs = []
        for h in range(n_heads):
            sl = slice(h * 2 * hd, (h + 1) * 2 * hd)
            d = o[2 * h:2 * h + 1, sl] - sc_ref[0] * o[2 * h + 1:2 * h + 2, sl]
            outs.append(_rms(d, sub_ref[...], SUBLN_EPS) * sc_ref[1])
        o_ref[0] = jnp.concatenate(outs, axis=-1).astype(o_ref.dtype)


def _diff_attn_sample(pb, cache_k, cache_v, page_table, layer, rel_bias, scal, subln):
    nbs, cols = pb.shape
    n_heads = rel_bias.shape[1]
    width = cols // 3
    hw = width // n_heads
    n_pool, depth, page = cache_k.shape[0], cache_k.shape[1], cache_k.shape[2]
    n_pages = page_table.shape[1]
    assert MAX_DISTANCE <= page
    ck = cache_k.reshape(n_pool, depth, page, width)
    cv = cache_v.reshape(n_pool, depth, page, width)
    pb3 = pb.reshape(nbs, 1, cols)
    rb_maps = jnp.repeat(rel_bias, 2, axis=1)
    grid_spec = pltpu.PrefetchScalarGridSpec(
        num_scalar_prefetch=1,
        grid=(nbs, n_pages),
        in_specs=[pl.BlockSpec(memory_space=pltpu.SMEM),
                  pl.BlockSpec((1, 1, width), lambda b, p, pt: (b, 0, 0)),
                  pl.BlockSpec((1, 1, width), lambda b, p, pt: (b, 0, 1)),
                  pl.BlockSpec((1, 1, width), lambda b, p, pt: (b, 0, 2)),
                  pl.BlockSpec((1, 1, page, width), lambda b, p, pt: (pt[b * n_pages + p], layer, 0, 0)),
                  pl.BlockSpec((1, 1, page, width), lambda b, p, pt: (pt[b * n_pages + p], layer, 0, 0)),
                  pl.BlockSpec(rb_maps.shape, lambda b, p, pt: (0, 0)),
                  pl.BlockSpec((1, hw), lambda b, p, pt: (0, 0))],
        out_specs=pl.BlockSpec((1, 1, width), lambda b, p, pt: (b, 0, 0)),
        scratch_shapes=[pltpu.VMEM((1, 2 * n_heads), F32), pltpu.VMEM((1, 2 * n_heads), F32),
                        pltpu.VMEM((2 * n_heads, width), F32), pltpu.VMEM((page, 2 * n_heads), F32)])
    out = pl.pallas_call(
        functools.partial(_paged_attn_kernel, n_heads=n_heads, hd=hw // 2, page=page),
        out_shape=jax.ShapeDtypeStruct((nbs, 1, width), BF16),
        grid_spec=grid_spec,
        compiler_params=_params("parallel", "arbitrary"),
        name="diff_attn_paged",
    )(page_table.reshape(-1), scal, pb3, pb3, pb3, ck, cv, rb_maps, subln)
    return out.reshape(nbs, width)


def _mem_attn_kernel(q_ref, k_ref, v_ref, o_ref, *, n_heads, hd):
    q = q_ref[0] * (hd ** -0.5)
    k = k_ref[0]
    v = v_ref[0]
    outs = []
    for h in range(n_heads):
        sl = slice(h * hd, (h + 1) * hd)
        s = _dot_nt(q[:, sl], k[:, sl])
        s = s - jnp.max(s, axis=-1, keepdims=True)
        p = jnp.exp(s)
        a = p / jnp.sum(p, axis=-1, keepdims=True)
        outs.append(_dot(a, v[:, sl]))
    o_ref[0] = jnp.concatenate(outs, axis=-1).astype(o_ref.dtype)


def _mem_attn(q3, mk, mv, n_heads, tq=512):
    nb, t, w = q3.shape
    nm = mk.shape[1]
    tq = _tile(t, tq)
    return pl.pallas_call(
        functools.partial(_mem_attn_kernel, n_heads=n_heads, hd=w // n_heads),
        out_shape=jax.ShapeDtypeStruct((nb, t, w), BF16),
        grid=(nb, t // tq),
        in_specs=[pl.BlockSpec((1, tq, w), lambda b, i: (b, i, 0)),
                  pl.BlockSpec((1, nm, w), lambda b, i: (b, 0, 0)),
                  pl.BlockSpec((1, nm, w), lambda b, i: (b, 0, 0))],
        out_specs=pl.BlockSpec((1, tq, w), lambda b, i: (b, i, 0)),
        compiler_params=_params("parallel", "arbitrary"),
        name="mem_attn",
    )(q3, mk, mv)


def _layer(lp, layer, x, mem_k, mem_v, a_shift, a_wkv, c_conv, c_delta, paged):
    nb, t, d = x.shape
    m = nb * t
    x2 = x.reshape(m, d)
    g = lp["g_pre_mix"]
    pa = _mm(x2, lp["w_in_a"], g, tn=896)
    pb = _mm(x2, lp["w_in_b"], g)
    pc = _mm(x2, lp["w_in_c"], g)
    ps = _mm(x2, lp["w_in_s"], g)
    pg = _mm(x2, lp["w_in_g"], g)

    chunk = CHUNK if t >= CHUNK else SUBLANES
    pa3 = pa.reshape(nb, t, -1)
    ya, new_wkv = _rwkv(pa3, a_shift, a_wkv, lp, chunk)
    new_shift = pa3[:, t - 1]

    n_heads_b = lp["rel_bias"].shape[1]
    w_b = pb.shape[1] // 3
    if paged is None:
        yb = _diff_attn_prompt(pb.reshape(nb, t, -1), lp["rel_bias"], lp["b_scal"], lp["b_subln"])
    else:
        cache_k, cache_v, page_table = paged
        yb = _diff_attn_sample(pb, cache_k, cache_v, page_table, layer, lp["rel_bias"], lp["b_scal"], lp["b_subln"])
    new_bk = pb[:, w_b:2 * w_b].reshape(nb, t, n_heads_b, -1)
    new_bv = pb[:, 2 * w_b:3 * w_b].reshape(nb, t, n_heads_b, -1)

    pc3 = pc.reshape(nb, t, -1)
    w_c3 = c_conv.shape[2]
    yc, new_delta = _gdn(pc3, ps.reshape(nb, t, -1), c_conv, c_delta, lp, chunk)
    new_conv = jnp.concatenate([c_conv, pc3[:, :, 0:w_c3]], axis=1)[:, t:]

    x2 = _merge(ya.reshape(m, -1), yb.reshape(m, -1), yc.reshape(m, -1), pg, x2,
                lp["w_br_a"], lp["w_br_b"], lp["w_br_c"], lp["w_out"], lp["g_post_mix"])

    n_heads_m = mem_k.shape[2]
    q = _mm(x2, lp["w_mem_q"], lp["g_pre_mem"])
    att = _mem_attn(q.reshape(nb, t, -1), mem_k.reshape(nb, mem_k.shape[1], -1),
                    mem_v.reshape(nb, mem_v.shape[1], -1), n_heads_m)
    x2 = _mm_post(att.reshape(m, -1), lp["w_mem_o"], lp["g_post_mem"], x2)

    hf = _mm(x2, lp["w_ffn1"], lp["g_pre_ffn"], act="relu2", out_dtype=BF16)
    x2 = _mm_post(hf, lp["w_ffn2"], lp["g_post_ffn"], x2)
    return x2.reshape(nb, t, d), (new_shift, new_wkv, new_conv, new_delta, new_bk, new_bv)


def _layer_params(p, l):
    d = p["w_in"].shape[1]
    n_ha, hd_a = p["a_rk"].shape[1], p["a_rk"].shape[2]
    w_a = n_ha * hd_a
    a_cols = 3 * w_a + LORA_W + LORA_A + LORA_G
    w_b = p["w_br_b"].shape[1]
    w_c = p["w_br_c"].shape[1]
    n_hc = p["c_a_log"].shape[1]
    b_cols = 3 * w_b
    row = lambda v: v.reshape(1, -1).astype(F32)
    w_in = p["w_in"][l]
    o_b = a_cols
    o_c = o_b + b_cols
    o_s = o_c + 4 * w_c
    o_g = o_s + 2 * n_hc
    lam_init = 0.8 - 0.6 * math.exp(-0.3 * l)
    lam = (jnp.exp(jnp.sum(p["b_lam_q1"][l] * p["b_lam_k1"][l]))
           - jnp.exp(jnp.sum(p["b_lam_q2"][l] * p["b_lam_k2"][l])) + lam_init)
    pad_heads = lambda v: jnp.zeros((1, LANES), F32).at[0, n_hc:2 * n_hc].set(v)
    lp = {
        "w_in_a": w_in[:, 0:o_b].astype(BF16),
        "w_in_b": w_in[:, o_b:o_c].astype(BF16),
        "w_in_c": w_in[:, o_c:o_s].astype(BF16),
        "w_in_s": jnp.pad(w_in[:, o_s:o_g], ((0, 0), (0, LANES - 2 * n_hc))).astype(BF16),
        "w_in_g": w_in[:, o_g:o_g + 3 * d].astype(BF16),
        "a_mu": row(p["a_mu"][l]), "a_w0": row(p["a_w0"][l]), "a_w2": p["a_w2"][l].astype(BF16),
        "a_a0": row(p["a_a0"][l]), "a_a2": p["a_a2"][l].astype(BF16), "a_g2": p["a_g2"][l].astype(BF16),
        "a_kk": row(p["a_kk"][l]), "a_ka": row(p["a_ka"][l]), "a_rk": row(p["a_rk"][l]),
        "a_ln_w": row(p["a_ln_w"][l]), "a_ln_b": row(p["a_ln_b"][l]),
        "rel_bias": p["rel_bias"].astype(F32),
        "b_scal": jnp.stack([lam, jnp.asarray(1.0 - lam_init, F32)]).astype(F32),
        "b_subln": row(p["b_subln"][l]),
        "c_conv_w": p["c_conv_w"][l].astype(F32),
        "c_a_log_pad": pad_heads(p["c_a_log"][l]), "c_dt_bias_pad": pad_heads(p["c_dt_bias"][l]),
        "c_norm_w": row(p["c_norm_w"][l]),
        "w_br_a": p["w_br_a"][l].astype(BF16), "w_br_b": p["w_br_b"][l].astype(BF16),
        "w_br_c": p["w_br_c"][l].astype(BF16), "w_out": p["w_out"][l].astype(BF16),
        "g_mem": row(p["g_mem"][l]), "w_mem_q": p["w_mem_q"][l].astype(BF16),
        "w_mem_kv": p["w_mem_kv"][l].astype(BF16), "w_mem_o": p["w_mem_o"][l].astype(BF16),
        "w_ffn1": p["w_ffn1"][l].astype(BF16), "w_ffn2": p["w_ffn2"][l].astype(BF16),
    }
    for name in ("g_pre_mix", "g_post_mix", "g_pre_mem", "g_post_mem", "g_pre_ffn", "g_post_ffn"):
        lp[name] = row(p[name][l])
    return lp


def kernel(x_prompt, x_sample, mem_prompt, cache_b_k, cache_b_v, page_table, state_a_wkv, state_a_shift, state_c_delta, state_c_conv, cache_mem_k, cache_mem_v, g_pre_mix, g_post_mix, g_pre_mem, g_post_mem, g_pre_ffn, g_post_ffn, w_in, a_mu, a_w0, a_w2, a_a0, a_a2, a_g2, a_kk, a_ka, a_rk, a_ln_w, a_ln_b, b_lam_q1, b_lam_k1, b_lam_q2, b_lam_k2, b_subln, rel_bias, c_conv_w, c_a_log, c_dt_bias, c_norm_w, w_br_a, w_br_b, w_br_c, w_out, g_mem, w_mem_q, w_mem_kv, w_mem_o, w_ffn1, w_ffn2):
    p = dict(g_pre_mix=g_pre_mix, g_post_mix=g_post_mix, g_pre_mem=g_pre_mem, g_post_mem=g_post_mem,
             g_pre_ffn=g_pre_ffn, g_post_ffn=g_post_ffn, w_in=w_in,
             a_mu=a_mu, a_w0=a_w0, a_w2=a_w2, a_a0=a_a0, a_a2=a_a2, a_g2=a_g2, a_kk=a_kk, a_ka=a_ka,
             a_rk=a_rk, a_ln_w=a_ln_w, a_ln_b=a_ln_b,
             b_lam_q1=b_lam_q1, b_lam_k1=b_lam_k1, b_lam_q2=b_lam_q2, b_lam_k2=b_lam_k2,
             b_subln=b_subln, rel_bias=rel_bias,
             c_conv_w=c_conv_w, c_a_log=c_a_log, c_dt_bias=c_dt_bias, c_norm_w=c_norm_w,
             w_br_a=w_br_a, w_br_b=w_br_b, w_br_c=w_br_c, w_out=w_out,
             g_mem=g_mem, w_mem_q=w_mem_q, w_mem_kv=w_mem_kv, w_mem_o=w_mem_o,
             w_ffn1=w_ffn1, w_ffn2=w_ffn2)
    depth = w_in.shape[0]
    nbp = x_prompt.shape[0]
    n_ha, hd_a = a_rk.shape[1], a_rk.shape[2]
    a_cols = a_mu.shape[1]
    n_hc = c_a_log.shape[1]
    hd_c = c_norm_w.shape[1]
    n_hm = cache_mem_k.shape[3]
    n_mem = mem_prompt.shape[1]
    taps = c_conv_w.shape[1] - 1

    xp, xs = x_prompt, x_sample
    outs_p, outs_s, mem_kv = [], [], []
    for l in range(depth):
        lp = _layer_params(p, l)
        mkv = _mm(mem_prompt.reshape(nbp * n_mem, -1), lp["w_mem_kv"], lp["g_mem"])
        w_mem = mkv.shape[1] // 2
        mk = mkv[:, 0:w_mem].reshape(nbp, n_mem, n_hm, -1)
        mv = mkv[:, w_mem:2 * w_mem].reshape(nbp, n_mem, n_hm, -1)
        mem_kv.append((mk, mv))
        xp, st = _layer(lp, l, xp, mk, mv,
                        jnp.zeros((nbp, a_cols), F32), jnp.zeros((nbp, n_ha, hd_a, hd_a), F32),
                        jnp.zeros((nbp, taps, 3 * n_hc * hd_c), F32), jnp.zeros((nbp, n_hc, hd_c, hd_c), F32),
                        None)
        outs_p.append(st)
        xs, st = _layer(lp, l, xs, cache_mem_k[l], cache_mem_v[l],
                        state_a_shift[l], state_a_wkv[l], state_c_conv[l], state_c_delta[l],
                        (cache_b_k, cache_b_v, page_table))
        outs_s.append(st)

    stack = lambda group, i, axis: jnp.stack([st[i] for st in group], axis=axis)
    return (xp, xs,
            stack(outs_p, 4, 1), stack(outs_p, 5, 1), stack(outs_s, 4, 1), stack(outs_s, 5, 1),
            stack(outs_p, 1, 0), stack(outs_s, 1, 0), stack(outs_p, 0, 0), stack(outs_s, 0, 0),
            stack(outs_p, 3, 0), stack(outs_s, 3, 0), stack(outs_p, 2, 0), stack(outs_s, 2, 0),
            jnp.stack([m[0] for m in mem_kv], axis=0), jnp.stack([m[1] for m in mem_kv], axis=0))
```

```python
import functools
import math

import jax
import jax.numpy as jnp
from jax import lax
from jax.experimental import pallas as pl
from jax.experimental.pallas import tpu as pltpu

F32 = jnp.float32
BF16 = jnp.bfloat16

RMS_EPS = 1e-6
GN_EPS_A = 64e-5
SUBLN_EPS = 1e-5
N_BUCKETS = 32
MAX_DISTANCE = 128
CONV_W = 4
LORA_W = 64
LORA_A = 64
LORA_G = 128

LANES = 128
SUBLANES = 8
VMEM_LIMIT_BYTES = 56 * 1024 * 1024

CHUNK = 64
ATT_BLOCK = 256
NEG_INF = -1e30


def _params(*sem):
    return pltpu.CompilerParams(dimension_semantics=sem, vmem_limit_bytes=VMEM_LIMIT_BYTES)


def _tile(n, pref):
    if n <= pref:
        return n
    for t in range(pref, 7, -1):
        if n % t == 0 and t % SUBLANES == 0:
            return t
    return n


def _dot(a, b):
    return jnp.dot(a.astype(BF16), b.astype(BF16), preferred_element_type=F32)


def _dot_nt(a, b):
    return lax.dot_general(a.astype(BF16), b.astype(BF16), (((1,), (1,)), ((), ())), preferred_element_type=F32)


def _dot_tn(a, b):
    return lax.dot_general(a.astype(BF16), b.astype(BF16), (((0,), (0,)), ((), ())), preferred_element_type=F32)


def _dot_f32(a, b):
    return jnp.dot(a, b, preferred_element_type=F32, precision=lax.Precision.HIGHEST)


def _rms(x, g, eps):
    return x * lax.rsqrt(jnp.mean(x * x, axis=-1, keepdims=True) + eps) * g


def _tri_masks(n):
    r = lax.broadcasted_iota(jnp.int32, (n, n), 0)
    c = lax.broadcasted_iota(jnp.int32, (n, n), 1)
    return r >= c, r > c


def _block_pair_masks(n):
    r = lax.broadcasted_iota(jnp.int32, (n, n), 0)
    c = lax.broadcasted_iota(jnp.int32, (n, n), 1)
    masks = []
    s = 1
    while s < n:
        masks.append(((r // (2 * s)) == (c // (2 * s))) & ((r // s) % 2 == 1) & ((c // s) % 2 == 0))
        s *= 2
    return masks


def _unit_lower_inverse_minus_eye(nmats, masks):
    qs = [jnp.where(masks[0], n, 0.0) for n in nmats]
    for mask in masks[1:]:
        n_s = [jnp.where(mask, n, 0.0) for n in nmats]
        ys = [n + _dot(q, n) for q, n in zip(qs, n_s)]
        qs = [q + y + _dot(y, q) for q, y in zip(qs, ys)]
    return qs


def _mm_kernel(x_ref, g_ref, w_ref, o_ref, xn_ref, *, norm, act):
    @pl.when(pl.program_id(1) == 0)
    def _():
        x = x_ref[...].astype(F32)
        if norm:
            x = _rms(x, g_ref[...], RMS_EPS)
        xn_ref[...] = x.astype(BF16)

    y = jnp.dot(xn_ref[...], w_ref[...], preferred_element_type=F32)
    if act == "relu2":
        y = jnp.square(jnp.maximum(y, 0.0))
    o_ref[...] = y.astype(o_ref.dtype)


def _mm(x, w, g=None, act=None, out_dtype=F32, tm=512, tn=512):
    m, k = x.shape
    n = w.shape[1]
    tm = _tile(m, tm)
    tn = _tile(n, tn)
    norm = g is not None
    if g is None:
        g = jnp.ones((1, k), F32)
    return pl.pallas_call(
        functools.partial(_mm_kernel, norm=norm, act=act),
        out_shape=jax.ShapeDtypeStruct((m, n), out_dtype),
        grid=(m // tm, n // tn),
        in_specs=[pl.BlockSpec((tm, k), lambda i, j: (i, 0)),
                  pl.BlockSpec((1, k), lambda i, j: (0, 0)),
                  pl.BlockSpec((k, tn), lambda i, j: (0, j))],
        out_specs=pl.BlockSpec((tm, tn), lambda i, j: (i, j)),
        scratch_shapes=[pltpu.VMEM((tm, k), BF16)],
        compiler_params=_params("parallel", "arbitrary"),
        name="mm_norm",
    )(x, g, w)


def _mm_post_kernel(y_ref, w_ref, g_ref, res_ref, o_ref, acc_ref):
    kk = pl.program_id(1)

    @pl.when(kk == 0)
    def _():
        acc_ref[...] = jnp.zeros_like(acc_ref)

    acc_ref[...] += jnp.dot(y_ref[...].astype(BF16), w_ref[...], preferred_element_type=F32)

    @pl.when(kk == pl.num_programs(1) - 1)
    def _():
        o_ref[...] = res_ref[...] + _rms(acc_ref[...], g_ref[...], RMS_EPS)


def _mm_post(y, w, g, res, tm=256, tk=1024):
    m, k = y.shape
    n = w.shape[1]
    tm = _tile(m, tm)
    tk = _tile(k, tk)
    return pl.pallas_call(
        _mm_post_kernel,
        out_shape=jax.ShapeDtypeStruct((m, n), F32),
        grid=(m // tm, k // tk),
        in_specs=[pl.BlockSpec((tm, tk), lambda i, j: (i, j)),
                  pl.BlockSpec((tk, n), lambda i, j: (j, 0)),
                  pl.BlockSpec((1, n), lambda i, j: (0, 0)),
                  pl.BlockSpec((tm, n), lambda i, j: (i, 0))],
        out_specs=pl.BlockSpec((tm, n), lambda i, j: (i, 0)),
        scratch_shapes=[pltpu.VMEM((tm, n), F32)],
        compiler_params=_params("parallel", "arbitrary"),
        name="mm_post",
    )(y, w, g, res)


def _merge_kernel(ya_ref, yb_ref, yc_ref, pg_ref, x_ref, wa_ref, wb_ref, wc_ref, wo_ref, g_ref, o_ref):
    d = x_ref.shape[1]
    gates = jax.nn.sigmoid(pg_ref[...])
    merged = (gates[:, 0:d] * jnp.dot(ya_ref[...], wa_ref[...], preferred_element_type=F32)
              + gates[:, d:2 * d] * jnp.dot(yb_ref[...], wb_ref[...], preferred_element_type=F32)
              + gates[:, 2 * d:3 * d] * jnp.dot(yc_ref[...], wc_ref[...], preferred_element_type=F32))
    out = jnp.dot(merged.astype(BF16), wo_ref[...], preferred_element_type=F32)
    o_ref[...] = x_ref[...] + _rms(out, g_ref[...], RMS_EPS)


def _merge(ya, yb, yc, pg, x, wa, wb, wc, wo, g, tm=256):
    m, d = x.shape
    tm = _tile(m, tm)
    row = lambda i: (i, 0)
    full = lambda i: (0, 0)
    return pl.pallas_call(
        _merge_kernel,
        out_shape=jax.ShapeDtypeStruct((m, d), F32),
        grid=(m // tm,),
        in_specs=[pl.BlockSpec((tm, ya.shape[1]), row), pl.BlockSpec((tm, yb.shape[1]), row),
                  pl.BlockSpec((tm, yc.shape[1]), row), pl.BlockSpec((tm, 3 * d), row),
                  pl.BlockSpec((tm, d), row),
                  pl.BlockSpec(wa.shape, full), pl.BlockSpec(wb.shape, full), pl.BlockSpec(wc.shape, full),
                  pl.BlockSpec(wo.shape, full), pl.BlockSpec((1, d), full)],
        out_specs=pl.BlockSpec((tm, d), row),
        compiler_params=_params("parallel"),
        name="merge",
    )(ya, yb, yc, pg, x, wa, wb, wc, wo, g)


def _rwkv_kernel(pa_ref, sh_ref, s0_ref, mu_ref, w0_ref, w2_ref, a0_ref, a2_ref, g2_ref, kk_ref, ka_ref, rk_ref,
                 lnw_ref, lnb_ref, y_ref, sout_ref, state_ref, carry_ref, *, chunk, t_valid, n_heads, hd):
    c = pl.program_id(1)
    width = n_heads * hd

    @pl.when(c == 0)
    def _():
        state_ref[...] = s0_ref[0]
        carry_ref[...] = sh_ref[0]

    pa = pa_ref[0]
    row = lax.broadcasted_iota(jnp.int32, (chunk, 1), 0)
    prev = jnp.where(row == 0, carry_ref[...], pltpu.roll(pa, 1, axis=0))
    carry_ref[...] = pa[chunk - 1:chunk, :]
    xa = pa + (prev - pa) * mu_ref[...]
    ar = xa[:, 0:width]
    ak = xa[:, width:2 * width]
    av = xa[:, 2 * width:3 * width]
    o = 3 * width
    aw = xa[:, o:o + LORA_W]
    aa = xa[:, o + LORA_W:o + LORA_W + LORA_A]
    ag = xa[:, o + LORA_W + LORA_A:o + LORA_W + LORA_A + LORA_G]

    w_raw = w0_ref[...] + _dot(jnp.tanh(aw), w2_ref[...])
    log_w = -jnp.exp(-jax.nn.softplus(-w_raw) - 0.5)
    a_in = jax.nn.sigmoid(a0_ref[...] + _dot(aa, a2_ref[...]))
    g_out = _dot(jax.nn.sigmoid(ag), g2_ref[...])
    kk_all = ak * kk_ref[...]
    k_all = ak * (1.0 + (a_in - 1.0) * ka_ref[...])
    rk_all = ar * k_all * rk_ref[...]

    masked = t_valid % chunk != 0
    valid = (row + c * chunk) < t_valid
    tri, strict = _tri_masks(chunk)
    pair_masks = _block_pair_masks(chunk)
    tri_f = tri.astype(F32)
    ln_w = lnw_ref[...]
    ln_b = lnb_ref[...]

    heads = range(n_heads)
    sls = [slice(h * hd, (h + 1) * hd) for h in heads]
    lo, hi = slice(0, chunk), slice(chunk, 2 * chunk)

    def prep(sl):
        kk = kk_all[:, sl]
        kk = kk * lax.rsqrt(jnp.sum(kk * kk, axis=-1, keepdims=True) + 1e-12)
        k = k_all[:, sl]
        b_vec = kk * a_in[:, sl]
        lw = log_w[:, sl]
        if masked:
            k = jnp.where(valid, k, 0.0)
            b_vec = jnp.where(valid, b_vec, 0.0)
            lw = jnp.where(valid, lw, 0.0)
        return -kk, b_vec, k, lw

    a_vecs, b_vecs, ks, lws = zip(*[prep(sl) for sl in sls])
    rs = [ar[:, sl] for sl in sls]
    vs = [av[:, sl] for sl in sls]
    g_incs = [_dot_f32(tri_f, lw) for lw in lws]
    g_lasts = [g[chunk - 1:chunk, :] for g in g_incs]
    x_mats, y_mats = [], []
    for h in heads:
        e_neg = jnp.exp(-g_incs[h])
        x_mats.append(jnp.concatenate([a_vecs[h] * jnp.exp(g_incs[h] - lws[h]), rs[h] * jnp.exp(g_incs[h])], axis=0))
        y_mats.append(jnp.concatenate([b_vecs[h] * e_neg, ks[h] * e_neg], axis=0))
    mms = [_dot_nt(x_mats[h], y_mats[h]) for h in heads]
    s0s = [state_ref[h] for h in heads]
    xss = [_dot_nt(x_mats[h], s0s[h]) for h in heads]
    t_qs = _unit_lower_inverse_minus_eye([jnp.where(strict, m[lo, lo], 0.0) for m in mms], pair_masks)
    rhss = [xss[h][lo] + _dot(jnp.where(strict, mms[h][lo, hi], 0.0), vs[h]) for h in heads]
    us = [rhss[h] + _dot(t_qs[h], rhss[h]) for h in heads]
    ys = [xss[h][hi] + _dot(jnp.where(tri, mms[h][hi, lo], 0.0), us[h])
          + _dot(jnp.where(tri, mms[h][hi, hi], 0.0), vs[h]) for h in heads]
    for h in heads:
        e_tail = jnp.exp(g_lasts[h] - g_incs[h])
        state_ref[h] = (s0s[h] * jnp.exp(g_lasts[h])
                        + _dot_tn(jnp.concatenate([us[h], vs[h]], axis=0),
                                  jnp.concatenate([b_vecs[h] * e_tail, ks[h] * e_tail], axis=0)))

    outs = []
    for h in heads:
        y, sl = ys[h], sls[h]
        mean = jnp.mean(y, axis=-1, keepdims=True)
        var = jnp.mean(jnp.square(y - mean), axis=-1, keepdims=True)
        yn = (y - mean) * lax.rsqrt(var + GN_EPS_A) * ln_w[:, sl] + ln_b[:, sl]
        bonus = jnp.sum(rk_all[:, sl], axis=-1, keepdims=True) * vs[h]
        outs.append((yn + bonus) * g_out[:, sl])

    y_ref[0] = jnp.concatenate(outs, axis=-1).astype(y_ref.dtype)

    @pl.when(c == pl.num_programs(1) - 1)
    def _():
        sout_ref[0] = state_ref[...]


def _rwkv(pa3, a_shift, wkv0, lp, chunk):
    nb, t, cols = pa3.shape
    n_heads, hd = wkv0.shape[1], wkv0.shape[2]
    width = n_heads * hd
    tp = -(-t // chunk) * chunk
    if tp != t:
        pa3 = jnp.pad(pa3, ((0, 0), (0, tp - t), (0, 0)))
    full2 = lambda b, c: (0, 0)
    vec = lambda n: pl.BlockSpec((1, n), full2)
    y, s_new = pl.pallas_call(
        functools.partial(_rwkv_kernel, chunk=chunk, t_valid=t, n_heads=n_heads, hd=hd),
        out_shape=(jax.ShapeDtypeStruct((nb, tp, width), BF16),
                   jax.ShapeDtypeStruct(wkv0.shape, F32)),
        grid=(nb, tp // chunk),
        in_specs=[pl.BlockSpec((1, chunk, cols), lambda b, c: (b, c, 0)),
                  pl.BlockSpec((1, 1, cols), lambda b, c: (b, 0, 0)),
                  pl.BlockSpec((1, n_heads, hd, hd), lambda b, c: (b, 0, 0, 0)),
                  vec(cols), vec(width), pl.BlockSpec((LORA_W, width), full2),
                  vec(width), pl.BlockSpec((LORA_A, width), full2), pl.BlockSpec((LORA_G, width), full2),
                  vec(width), vec(width), vec(width), vec(width), vec(width)],
        out_specs=(pl.BlockSpec((1, chunk, width), lambda b, c: (b, c, 0)),
                   pl.BlockSpec((1, n_heads, hd, hd), lambda b, c: (b, 0, 0, 0))),
        scratch_shapes=[pltpu.VMEM((n_heads, hd, hd), F32), pltpu.VMEM((1, cols), F32)],
        compiler_params=_params("parallel", "arbitrary"),
        name="rwkv7",
    )(pa3, a_shift.reshape(nb, 1, cols), wkv0, lp["a_mu"], lp["a_w0"], lp["a_w2"], lp["a_a0"], lp["a_a2"],
      lp["a_g2"], lp["a_kk"], lp["a_ka"], lp["a_rk"], lp["a_ln_w"], lp["a_ln_b"])
    return y[:, :t], s_new


def _gdn_kernel(pc_ref, ps_ref, cv_ref, s0_ref, cw_ref, alog_ref, dtb_ref, nw_ref,
                y_ref, sout_ref, state_ref, carry_ref, *, chunk, t_valid, n_heads, hd):
    c = pl.program_id(1)
    width = n_heads * hd
    taps = CONV_W - 1

    @pl.when(c == 0)
    def _():
        state_ref[...] = s0_ref[0]
        carry_ref[...] = cv_ref[0]

    pc = pc_ref[0]
    x = pc[:, 0:3 * width]
    cz = pc[:, 3 * width:4 * width]
    row = lax.broadcasted_iota(jnp.int32, (chunk, 1), 0)
    conv = x * cw_ref[taps:taps + 1, :]
    for j in range(1, CONV_W):
        shifted = pltpu.roll(x, j, axis=0)
        for i in range(j):
            shifted = jnp.where(row == i, carry_ref[taps - j + i:taps - j + i + 1, :], shifted)
        conv = conv + shifted * cw_ref[taps - j:taps - j + 1, :]
    carry_ref[...] = x[chunk - taps:chunk, :]
    act = jax.nn.silu(conv)
    q_all = act[:, 0:width]
    k_all = act[:, width:2 * width]
    v_all = act[:, 2 * width:3 * width]

    valid = (row + c * chunk) < t_valid
    ps = ps_ref[0]
    beta_all = jnp.where(valid, jax.nn.sigmoid(ps), 0.0)
    lane = lax.broadcasted_iota(jnp.int32, (1, ps.shape[1]), 1)
    dt_full = dtb_ref[...]
    neg_rate = -jnp.exp(alog_ref[...])
    g_all = jnp.where(valid & (lane >= n_heads) & (lane < 2 * n_heads),
                      neg_rate * jax.nn.softplus(ps + dt_full), 0.0)
    tri, strict = _tri_masks(chunk)
    pair_masks = _block_pair_masks(chunk)
    gc_all = _dot_f32(tri.astype(F32), g_all)
    gc_rows = gc_all.T

    heads = range(n_heads)
    sls = [slice(h * hd, (h + 1) * hd) for h in heads]
    lo, hi = slice(0, chunk), slice(chunk, 2 * chunk)
    qs, ks, vs, kbs, gcols, decays, e_gs, betas = [], [], [], [], [], [], [], []
    for h in heads:
        q = q_all[:, sls[h]]
        qs.append(q * lax.rsqrt(jnp.sum(q * q, axis=-1, keepdims=True) + 1e-6) * (hd ** -0.5))
        k = k_all[:, sls[h]]
        k = k * lax.rsqrt(jnp.sum(k * k, axis=-1, keepdims=True) + 1e-6)
        ks.append(k)
        vs.append(v_all[:, sls[h]])
        beta = beta_all[:, h:h + 1]
        betas.append(beta)
        kbs.append(k * beta)
        gcol = gc_all[:, n_heads + h:n_heads + h + 1]
        grow = gc_rows[n_heads + h:n_heads + h + 1, :]
        gcols.append(gcol)
        decays.append(jnp.where(tri, jnp.exp(jnp.where(tri, gcol - grow, 0.0)), 0.0))
        e_gs.append(jnp.exp(gcol))
    mms = [_dot_nt(jnp.concatenate([kbs[h], qs[h]], axis=0), ks[h]) for h in heads]
    t_qs = _unit_lower_inverse_minus_eye([jnp.where(strict, -(mms[h][lo] * decays[h]), 0.0) for h in heads],
                                         pair_masks)
    rhss = [jnp.concatenate([vs[h] * betas[h], kbs[h] * e_gs[h]], axis=1) for h in heads]
    uws = [rhss[h] + _dot(t_qs[h], rhss[h]) for h in heads]
    s0s = [state_ref[h] for h in heads]
    wss = [_dot(jnp.concatenate([uws[h][:, hd:2 * hd], qs[h] * e_gs[h]], axis=0), s0s[h]) for h in heads]
    v_news = [uws[h][:, 0:hd] - wss[h][lo] for h in heads]
    os_ = [wss[h][hi] + _dot(mms[h][hi] * decays[h], v_news[h]) for h in heads]
    for h in heads:
        g_last = gcols[h][chunk - 1:chunk, :]
        state_ref[h] = s0s[h] * jnp.exp(g_last) + _dot_tn(ks[h] * jnp.exp(g_last - gcols[h]), v_news[h])

    norm_w = nw_ref[...]
    outs = []
    for h in heads:
        o = os_[h]
        on = o * lax.rsqrt(jnp.mean(o * o, axis=-1, keepdims=True) + RMS_EPS) * norm_w
        outs.append(on * jax.nn.silu(cz[:, sls[h]]))

    y_ref[0] = jnp.concatenate(outs, axis=-1).astype(y_ref.dtype)

    @pl.when(c == pl.num_programs(1) - 1)
    def _():
        sout_ref[0] = state_ref[...]


def _gdn(pc3, ps3, c_conv, delta0, lp, chunk):
    nb, t, cols = pc3.shape
    n_heads, hd = delta0.shape[1], delta0.shape[2]
    width = n_heads * hd
    taps = CONV_W - 1
    tp = -(-t // chunk) * chunk
    if tp != t:
        pc3 = jnp.pad(pc3, ((0, 0), (0, tp - t), (0, 0)))
        ps3 = jnp.pad(ps3, ((0, 0), (0, tp - t), (0, 0)))
    full2 = lambda b, c: (0, 0)
    y, s_new = pl.pallas_call(
        functools.partial(_gdn_kernel, chunk=chunk, t_valid=t, n_heads=n_heads, hd=hd),
        out_shape=(jax.ShapeDtypeStruct((nb, tp, width), BF16),
                   jax.ShapeDtypeStruct(delta0.shape, F32)),
        grid=(nb, tp // chunk),
        in_specs=[pl.BlockSpec((1, chunk, cols), lambda b, c: (b, c, 0)),
                  pl.BlockSpec((1, chunk, LANES), lambda b, c: (b, c, 0)),
                  pl.BlockSpec((1, taps, 3 * width), lambda b, c: (b, 0, 0)),
                  pl.BlockSpec((1, n_heads, hd, hd), lambda b, c: (b, 0, 0, 0)),
                  pl.BlockSpec((CONV_W, 3 * width), full2),
                  pl.BlockSpec((1, LANES), full2), pl.BlockSpec((1, LANES), full2),
                  pl.BlockSpec((1, hd), full2)],
        out_specs=(pl.BlockSpec((1, chunk, width), lambda b, c: (b, c, 0)),
                   pl.BlockSpec((1, n_heads, hd, hd), lambda b, c: (b, 0, 0, 0))),
        scratch_shapes=[pltpu.VMEM((n_heads, hd, hd), F32), pltpu.VMEM((taps, 3 * width), F32)],
        compiler_params=_params("parallel", "arbitrary"),
        name="gdn",
    )(pc3, ps3, c_conv, delta0, lp["c_conv_w"], lp["c_a_log_pad"], lp["c_dt_bias_pad"], lp["c_norm_w"])
    return y[:, :t], s_new


def _t5_bucket(dist):
    n = jnp.maximum(dist, 0)
    max_exact = N_BUCKETS // 2
    nf = jnp.maximum(n, 1).astype(F32)
    large = max_exact + (jnp.log(nf / max_exact) / math.log(MAX_DISTANCE / max_exact)
                         * (N_BUCKETS - max_exact)).astype(jnp.int32)
    return jnp.where(n < max_exact, n, jnp.minimum(large, N_BUCKETS - 1))


def _diff_attn_kernel(rb_ref, sc_ref, q_ref, k_ref, v_ref, sub_ref, o_ref, bias_ref, kb_ref, vb_ref, *, blk, hd):
    h = pl.program_id(1)
    qi = pl.program_id(2)
    far_bias = rb_ref[N_BUCKETS - 1, h]

    @pl.when(qi == 0)
    def _():
        kb_ref[...] = k_ref[0].astype(BF16)
        vb_ref[...] = v_ref[0].astype(BF16)
        r = lax.broadcasted_iota(jnp.int32, (blk, blk), 0)
        c = lax.broadcasted_iota(jnp.int32, (blk, blk), 1)
        for delta in range(2):
            dist = delta * blk + r - c
            bucket = _t5_bucket(dist)
            bias = jnp.zeros((blk, blk), F32)
            for n in range(N_BUCKETS):
                bias = jnp.where(bucket == n, rb_ref[n, h], bias)
            bias_ref[delta] = jnp.where(dist >= 0, bias, NEG_INF)

    q = q_ref[0] * (hd ** -0.5)
    first_half = lax.broadcasted_iota(jnp.int32, (1, 2 * hd), 1) < hd
    qs = (jnp.where(first_half, q, 0.0).astype(BF16), jnp.where(first_half, 0.0, q).astype(BF16))

    def step(kj, bias, carry):
        start = pl.multiple_of(kj * blk, blk)
        kt = kb_ref[pl.ds(start, blk), :]
        vt = vb_ref[pl.ds(start, blk), :]
        new = []
        for comp in range(2):
            m, l, acc = carry[comp]
            s = _dot_nt(qs[comp], kt) + bias
            m_new = jnp.maximum(m, jnp.max(s, axis=-1, keepdims=True))
            alpha = jnp.exp(m - m_new)
            p = jnp.exp(s - m_new)
            l_new = alpha * l + jnp.sum(p, axis=-1, keepdims=True)
            acc_new = alpha * acc + jnp.dot(p.astype(BF16), vt, preferred_element_type=F32)
            new.append((m_new, l_new, acc_new))
        return tuple(new)

    init = tuple((jnp.full((blk, 1), NEG_INF, F32), jnp.zeros((blk, 1), F32), jnp.zeros((blk, 2 * hd), F32))
                 for _ in range(2))
    carry = lax.fori_loop(0, jnp.maximum(qi - 1, 0), lambda kj, cr: step(kj, far_bias, cr), init)
    carry = lax.cond(qi >= 1, lambda cr: step(qi - 1, bias_ref[1], cr), lambda cr: cr, carry)
    carry = step(qi, bias_ref[0], carry)

    (_, l1, acc1), (_, l2, acc2) = carry
    out = acc1 / l1 - sc_ref[0] * (acc2 / l2)
    out = _rms(out, sub_ref[...], SUBLN_EPS) * sc_ref[1]
    o_ref[0] = out.astype(o_ref.dtype)


def _diff_attn_prompt(pb3, rel_bias, scal, subln):
    nb, t, cols = pb3.shape
    n_heads = rel_bias.shape[1]
    hw = cols // (3 * n_heads)
    blk = ATT_BLOCK if t % ATT_BLOCK == 0 else MAX_DISTANCE
    assert t % blk == 0 and MAX_DISTANCE <= blk
    smem = pl.BlockSpec(memory_space=pltpu.SMEM)
    return pl.pallas_call(
        functools.partial(_diff_attn_kernel, blk=blk, hd=hw // 2),
        out_shape=jax.ShapeDtypeStruct((nb, t, n_heads * hw), BF16),
        grid=(nb, n_heads, t // blk),
        in_specs=[smem, smem,
                  pl.BlockSpec((1, blk, hw), lambda b, h, i: (b, i, h)),
                  pl.BlockSpec((1, t, hw), lambda b, h, i: (b, 0, n_heads + h)),
                  pl.BlockSpec((1, t, hw), lambda b, h, i: (b, 0, 2 * n_heads + h)),
                  pl.BlockSpec((1, hw), lambda b, h, i: (0, 0))],
        out_specs=pl.BlockSpec((1, blk, hw), lambda b, h, i: (b, i, h)),
        scratch_shapes=[pltpu.VMEM((2, blk, blk), F32), pltpu.VMEM((t, hw), BF16), pltpu.VMEM((t, hw), BF16)],
        compiler_params=_params("parallel", "parallel", "arbitrary"),
        name="diff_attn_prompt",
    )(rel_bias, scal, pb3, pb3, pb3, subln)


def _paged_attn_kernel(pt_ref, sc_ref, q_ref, kn_ref, vn_ref, rb_ref, sub_ref, *rest, n_heads, hd, page, n_pages):
    k_refs = rest[0:n_pages]
    v_refs = rest[n_pages:2 * n_pages]
    o_ref, bias_ref = rest[2 * n_pages], rest[2 * n_pages + 1]
    n_maps = 2 * n_heads
    rows = page * n_heads
    map_idx = lax.broadcasted_iota(jnp.int32, (n_maps, 1), 0)

    @pl.when(pl.program_id(0) == 0)
    def _():
        lane = lax.broadcasted_iota(jnp.int32, (1, rows), 1)
        own_head = (lane % n_heads) == (map_idx // 2)
        far = jnp.broadcast_to(rb_ref[:, N_BUCKETS - 1:N_BUCKETS], (n_maps, rows))
        bucket = _t5_bucket(page - lane // n_heads)
        near = jnp.zeros((n_maps, rows), F32)
        for n in range(N_BUCKETS):
            near = jnp.where(bucket == n, rb_ref[:, n:n + 1], near)
        bias_ref[0] = jnp.where(own_head, far, NEG_INF)
        bias_ref[1] = jnp.where(own_head, near, NEG_INF)

    own_half = (lax.broadcasted_iota(jnp.int32, (1, 2 * hd), 1) // hd) == (map_idx % 2)
    qm = jnp.where(own_half, q_ref[0] * (hd ** -0.5), 0.0)
    qb = qm.astype(BF16)

    scores = [_dot_nt(qb, k_refs[i][0, 0].astype(BF16)) + bias_ref[1 if i == n_pages - 1 else 0]
              for i in range(n_pages)]
    s_self = jnp.sum(qm * kn_ref[0], axis=1, keepdims=True) + rb_ref[:, 0:1]
    m = s_self
    for s in scores:
        m = jnp.maximum(m, jnp.max(s, axis=1, keepdims=True))
    p_self = jnp.exp(s_self - m)
    l = p_self
    acc = p_self * vn_ref[0]
    for i in range(n_pages):
        p = jnp.exp(scores[i] - m)
        l = l + jnp.sum(p, axis=1, keepdims=True)
        acc = acc + jnp.dot(p.astype(BF16), v_refs[i][0, 0].astype(BF16), preferred_element_type=F32)
    o = acc / l
    d = o - sc_ref[0] * pltpu.roll(o, n_maps - 1, axis=0)
    o_ref[0] = (_rms(d, sub_ref[...], SUBLN_EPS) * sc_ref[1]).astype(o_ref.dtype)


def _diff_attn_sample(pb, cache_k, cache_v, page_table, layer, rel_bias, scal, subln):
    nbs, cols = pb.shape
    n_heads = rel_bias.shape[1]
    width = cols // 3
    hw = width // n_heads
    n_pool, depth, page = cache_k.shape[0], cache_k.shape[1], cache_k.shape[2]
    n_pages = page_table.shape[1]
    assert MAX_DISTANCE <= page
    n_maps = 2 * n_heads
    rows = page * n_heads
    ck = cache_k.reshape(n_pool, depth, rows, hw)
    cv = cache_v.reshape(n_pool, depth, rows, hw)
    qkv = jnp.repeat(pb.reshape(nbs, 3, n_heads, hw), 2, axis=2)
    rb_maps = jnp.repeat(rel_bias, 2, axis=1).T
    per_seq = pl.BlockSpec((1, n_maps, hw), lambda b, pt: (b, 0, 0))
    page_spec = lambda i: pl.BlockSpec((1, 1, rows, hw), lambda b, pt: (pt[b * n_pages + i], layer, 0, 0))
    grid_spec = pltpu.PrefetchScalarGridSpec(
        num_scalar_prefetch=1,
        grid=(nbs,),
        in_specs=[pl.BlockSpec(memory_space=pltpu.SMEM), per_seq, per_seq, per_seq,
                  pl.BlockSpec(rb_maps.shape, lambda b, pt: (0, 0)),
                  pl.BlockSpec((1, hw), lambda b, pt: (0, 0))]
                 + [page_spec(i) for i in range(n_pages)] + [page_spec(i) for i in range(n_pages)],
        out_specs=per_seq,
        scratch_shapes=[pltpu.VMEM((2, n_maps, rows), F32)])
    out = pl.pallas_call(
        functools.partial(_paged_attn_kernel, n_heads=n_heads, hd=hw // 2, page=page, n_pages=n_pages),
        out_shape=jax.ShapeDtypeStruct((nbs, n_maps, hw), BF16),
        grid_spec=grid_spec,
        compiler_params=_params("arbitrary"),
        name="diff_attn_paged",
    )(page_table.reshape(-1), scal, qkv[:, 0], qkv[:, 1], qkv[:, 2], rb_maps, subln,
      *([ck] * n_pages), *([cv] * n_pages))
    return out[:, 0::2].reshape(nbs, width)


def _mem_attn_kernel(q_ref, k_ref, v_ref, o_ref, *, n_heads, hd):
    q = q_ref[0] * (hd ** -0.5)
    k = k_ref[0]
    v = v_ref[0]
    outs = []
    for h in range(n_heads):
        sl = slice(h * hd, (h + 1) * hd)
        s = _dot_nt(q[:, sl], k[:, sl])
        s = s - jnp.max(s, axis=-1, keepdims=True)
        p = jnp.exp(s)
        a = p / jnp.sum(p, axis=-1, keepdims=True)
        outs.append(_dot(a, v[:, sl]))
    o_ref[0] = jnp.concatenate(outs, axis=-1).astype(o_ref.dtype)


def _mem_attn(q3, mk, mv, n_heads, tq=512):
    nb, t, w = q3.shape
    nm = mk.shape[1]
    tq = _tile(t, tq)
    return pl.pallas_call(
        functools.partial(_mem_attn_kernel, n_heads=n_heads, hd=w // n_heads),
        out_shape=jax.ShapeDtypeStruct((nb, t, w), BF16),
        grid=(nb, t // tq),
        in_specs=[pl.BlockSpec((1, tq, w), lambda b, i: (b, i, 0)),
                  pl.BlockSpec((1, nm, w), lambda b, i: (b, 0, 0)),
                  pl.BlockSpec((1, nm, w), lambda b, i: (b, 0, 0))],
        out_specs=pl.BlockSpec((1, tq, w), lambda b, i: (b, i, 0)),
        compiler_params=_params("parallel", "arbitrary"),
        name="mem_attn",
    )(q3, mk, mv)


def _layer(lp, layer, x, mem_k, mem_v, a_shift, a_wkv, c_conv, c_delta, paged):
    nb, t, d = x.shape
    m = nb * t
    x2 = x.reshape(m, d)
    g = lp["g_pre_mix"]
    pa = _mm(x2, lp["w_in_a"], g, tn=896)
    pb = _mm(x2, lp["w_in_b"], g)
    pc = _mm(x2, lp["w_in_c"], g)
    ps = _mm(x2, lp["w_in_s"], g)
    pg = _mm(x2, lp["w_in_g"], g)

    chunk = CHUNK if t >= CHUNK else SUBLANES
    pa3 = pa.reshape(nb, t, -1)
    ya, new_wkv = _rwkv(pa3, a_shift, a_wkv, lp, chunk)
    new_shift = pa3[:, t - 1]

    n_heads_b = lp["rel_bias"].shape[1]
    w_b = pb.shape[1] // 3
    if paged is None:
        yb = _diff_attn_prompt(pb.reshape(nb, t, -1), lp["rel_bias"], lp["b_scal"], lp["b_subln"])
    else:
        cache_k, cache_v, page_table = paged
        yb = _diff_attn_sample(pb, cache_k, cache_v, page_table, layer, lp["rel_bias"], lp["b_scal"], lp["b_subln"])
    new_bk = pb[:, w_b:2 * w_b].reshape(nb, t, n_heads_b, -1)
    new_bv = pb[:, 2 * w_b:3 * w_b].reshape(nb, t, n_heads_b, -1)

    pc3 = pc.reshape(nb, t, -1)
    w_c3 = c_conv.shape[2]
    yc, new_delta = _gdn(pc3, ps.reshape(nb, t, -1), c_conv, c_delta, lp, chunk)
    new_conv = jnp.concatenate([c_conv, pc3[:, :, 0:w_c3]], axis=1)[:, t:]

    x2 = _merge(ya.reshape(m, -1), yb.reshape(m, -1), yc.reshape(m, -1), pg, x2,
                lp["w_br_a"], lp["w_br_b"], lp["w_br_c"], lp["w_out"], lp["g_post_mix"])

    n_heads_m = mem_k.shape[2]
    q = _mm(x2, lp["w_mem_q"], lp["g_pre_mem"])
    att = _mem_attn(q.reshape(nb, t, -1), mem_k.reshape(nb, mem_k.shape[1], -1),
                    mem_v.reshape(nb, mem_v.shape[1], -1), n_heads_m)
    x2 = _mm_post(att.reshape(m, -1), lp["w_mem_o"], lp["g_post_mem"], x2)

    hf = _mm(x2, lp["w_ffn1"], lp["g_pre_ffn"], act="relu2", out_dtype=BF16)
    x2 = _mm_post(hf, lp["w_ffn2"], lp["g_post_ffn"], x2)
    return x2.reshape(nb, t, d), (new_shift, new_wkv, new_conv, new_delta, new_bk, new_bv)


def _layer_params(p, l):
    d = p["w_in"].shape[1]
    n_ha, hd_a = p["a_rk"].shape[1], p["a_rk"].shape[2]
    w_a = n_ha * hd_a
    a_cols = 3 * w_a + LORA_W + LORA_A + LORA_G
    w_b = p["w_br_b"].shape[1]
    w_c = p["w_br_c"].shape[1]
    n_hc = p["c_a_log"].shape[1]
    b_cols = 3 * w_b
    row = lambda v: v.reshape(1, -1).astype(F32)
    w_in = p["w_in"][l]
    o_b = a_cols
    o_c = o_b + b_cols
    o_s = o_c + 4 * w_c
    o_g = o_s + 2 * n_hc
    lam_init = 0.8 - 0.6 * math.exp(-0.3 * l)
    lam = (jnp.exp(jnp.sum(p["b_lam_q1"][l] * p["b_lam_k1"][l]))
           - jnp.exp(jnp.sum(p["b_lam_q2"][l] * p["b_lam_k2"][l])) + lam_init)
    pad_heads = lambda v: jnp.zeros((1, LANES), F32).at[0, n_hc:2 * n_hc].set(v)
    lp = {
        "w_in_a": w_in[:, 0:o_b].astype(BF16),
        "w_in_b": w_in[:, o_b:o_c].astype(BF16),
        "w_in_c": w_in[:, o_c:o_s].astype(BF16),
        "w_in_s": jnp.pad(w_in[:, o_s:o_g], ((0, 0), (0, LANES - 2 * n_hc))).astype(BF16),
        "w_in_g": w_in[:, o_g:o_g + 3 * d].astype(BF16),
        "a_mu": row(p["a_mu"][l]), "a_w0": row(p["a_w0"][l]), "a_w2": p["a_w2"][l].astype(BF16),
        "a_a0": row(p["a_a0"][l]), "a_a2": p["a_a2"][l].astype(BF16), "a_g2": p["a_g2"][l].astype(BF16),
        "a_kk": row(p["a_kk"][l]), "a_ka": row(p["a_ka"][l]), "a_rk": row(p["a_rk"][l]),
        "a_ln_w": row(p["a_ln_w"][l]), "a_ln_b": row(p["a_ln_b"][l]),
        "rel_bias": p["rel_bias"].astype(F32),
        "b_scal": jnp.stack([lam, jnp.asarray(1.0 - lam_init, F32)]).astype(F32),
        "b_subln": row(p["b_subln"][l]),
        "c_conv_w": p["c_conv_w"][l].astype(F32),
        "c_a_log_pad": pad_heads(p["c_a_log"][l]), "c_dt_bias_pad": pad_heads(p["c_dt_bias"][l]),
        "c_norm_w": row(p["c_norm_w"][l]),
        "w_br_a": p["w_br_a"][l].astype(BF16), "w_br_b": p["w_br_b"][l].astype(BF16),
        "w_br_c": p["w_br_c"][l].astype(BF16), "w_out": p["w_out"][l].astype(BF16),
        "g_mem": row(p["g_mem"][l]), "w_mem_q": p["w_mem_q"][l].astype(BF16),
        "w_mem_kv": p["w_mem_kv"][l].astype(BF16), "w_mem_o": p["w_mem_o"][l].astype(BF16),
        "w_ffn1": p["w_ffn1"][l].astype(BF16), "w_ffn2": p["w_ffn2"][l].astype(BF16),
    }
    for name in ("g_pre_mix", "g_post_mix", "g_pre_mem", "g_post_mem", "g_pre_ffn", "g_post_ffn"):
        lp[name] = row(p[name][l])
    return lp


def kernel(x_prompt, x_sample, mem_prompt, cache_b_k, cache_b_v, page_table, state_a_wkv, state_a_shift, state_c_delta, state_c_conv, cache_mem_k, cache_mem_v, g_pre_mix, g_post_mix, g_pre_mem, g_post_mem, g_pre_ffn, g_post_ffn, w_in, a_mu, a_w0, a_w2, a_a0, a_a2, a_g2, a_kk, a_ka, a_rk, a_ln_w, a_ln_b, b_lam_q1, b_lam_k1, b_lam_q2, b_lam_k2, b_subln, rel_bias, c_conv_w, c_a_log, c_dt_bias, c_norm_w, w_br_a, w_br_b, w_br_c, w_out, g_mem, w_mem_q, w_mem_kv, w_mem_o, w_ffn1, w_ffn2):
    p = dict(g_pre_mix=g_pre_mix, g_post_mix=g_post_mix, g_pre_mem=g_pre_mem, g_post_mem=g_post_mem,
             g_pre_ffn=g_pre_ffn, g_post_ffn=g_post_ffn, w_in=w_in,
             a_mu=a_mu, a_w0=a_w0, a_w2=a_w2, a_a0=a_a0, a_a2=a_a2, a_g2=a_g2, a_kk=a_kk, a_ka=a_ka,
             a_rk=a_rk, a_ln_w=a_ln_w, a_ln_b=a_ln_b,
             b_lam_q1=b_lam_q1, b_lam_k1=b_lam_k1, b_lam_q2=b_lam_q2, b_lam_k2=b_lam_k2,
             b_subln=b_subln, rel_bias=rel_bias,
             c_conv_w=c_conv_w, c_a_log=c_a_log, c_dt_bias=c_dt_bias, c_norm_w=c_norm_w,
             w_br_a=w_br_a, w_br_b=w_br_b, w_br_c=w_br_c, w_out=w_out,
             g_mem=g_mem, w_mem_q=w_mem_q, w_mem_kv=w_mem_kv, w_mem_o=w_mem_o,
             w_ffn1=w_ffn1, w_ffn2=w_ffn2)
    depth = w_in.shape[0]
    nbp = x_prompt.shape[0]
    n_ha, hd_a = a_rk.shape[1], a_rk.shape[2]
    a_cols = a_mu.shape[1]
    n_hc = c_a_log.shape[1]
    hd_c = c_norm_w.shape[1]
    n_hm = cache_mem_k.shape[3]
    n_mem = mem_prompt.shape[1]
    taps = c_conv_w.shape[1] - 1

    xp, xs = x_prompt, x_sample
    outs_p, outs_s, mem_kv = [], [], []
    for l in range(depth):
        lp = _layer_params(p, l)
        mkv = _mm(mem_prompt.reshape(nbp * n_mem, -1), lp["w_mem_kv"], lp["g_mem"])
        w_mem = mkv.shape[1] // 2
        mk = mkv[:, 0:w_mem].reshape(nbp, n_mem, n_hm, -1)
        mv = mkv[:, w_mem:2 * w_mem].reshape(nbp, n_mem, n_hm, -1)
        mem_kv.append((mk, mv))
        xp, st = _layer(lp, l, xp, mk, mv,
                        jnp.zeros((nbp, a_cols), F32), jnp.zeros((nbp, n_ha, hd_a, hd_a), F32),
                        jnp.zeros((nbp, taps, 3 * n_hc * hd_c), F32), jnp.zeros((nbp, n_hc, hd_c, hd_c), F32),
                        None)
        outs_p.append(st)
        xs, st = _layer(lp, l, xs, cache_mem_k[l], cache_mem_v[l],
                        state_a_shift[l], state_a_wkv[l], state_c_conv[l], state_c_delta[l],
                        (cache_b_k, cache_b_v, page_table))
        outs_s.append(st)

    stack = lambda group, i, axis: jnp.stack([st[i] for st in group], axis=axis)
    return (xp, xs,
            stack(outs_p, 4, 1), stack(outs_p, 5, 1), stack(outs_s, 4, 1), stack(outs_s, 5, 1),
            stack(outs_p, 1, 0), stack(outs_s, 1, 0), stack(outs_p, 0, 0), stack(outs_s, 0, 0),
            stack(outs_p, 3, 0), stack(outs_s, 3, 0), stack(outs_p, 2, 0), stack(outs_s, 2, 0),
            jnp.stack([m[0] for m in mem_kv], axis=0), jnp.stack([m[1] for m in mem_kv], axis=0))
```

```python
import functools
import math

import jax
import jax.numpy as jnp
from jax import lax
from jax.experimental import pallas as pl
from jax.experimental.pallas import tpu as pltpu

F32 = jnp.float32
BF16 = jnp.bfloat16

RMS_EPS = 1e-6
GN_EPS_A = 64e-5
SUBLN_EPS = 1e-5
N_BUCKETS = 32
MAX_DISTANCE = 128
CONV_W = 4
LORA_W = 64
LORA_A = 64
LORA_G = 128

LANES = 128
SUBLANES = 8
VMEM_LIMIT_BYTES = 56 * 1024 * 1024

CHUNK = 64
ATT_BLOCK = 512
NEG_INF = -1e30


def _params(*sem):
    return pltpu.CompilerParams(dimension_semantics=sem, vmem_limit_bytes=VMEM_LIMIT_BYTES)


def _tile(n, pref):
    if n <= pref:
        return n
    for t in range(pref, 7, -1):
        if n % t == 0 and t % SUBLANES == 0:
            return t
    return n


def _dot(a, b):
    return jnp.dot(a.astype(BF16), b.astype(BF16), preferred_element_type=F32)


def _dot_nt(a, b):
    return lax.dot_general(a.astype(BF16), b.astype(BF16), (((1,), (1,)), ((), ())), preferred_element_type=F32)


def _dot_tn(a, b):
    return lax.dot_general(a.astype(BF16), b.astype(BF16), (((0,), (0,)), ((), ())), preferred_element_type=F32)


def _dot_f32(a, b):
    return jnp.dot(a, b, preferred_element_type=F32, precision=lax.Precision.HIGHEST)


def _rms(x, g, eps):
    return x * lax.rsqrt(jnp.mean(x * x, axis=-1, keepdims=True) + eps) * g


def _tri_masks(n):
    r = lax.broadcasted_iota(jnp.int32, (n, n), 0)
    c = lax.broadcasted_iota(jnp.int32, (n, n), 1)
    return r >= c, r > c


def _block_pair_masks(n):
    r = lax.broadcasted_iota(jnp.int32, (n, n), 0)
    c = lax.broadcasted_iota(jnp.int32, (n, n), 1)
    masks = []
    s = 1
    while s < n:
        masks.append(((r // (2 * s)) == (c // (2 * s))) & ((r // s) % 2 == 1) & ((c // s) % 2 == 0))
        s *= 2
    return masks


def _unit_lower_inverse_minus_eye(nmats, masks):
    qs = [jnp.where(masks[0], n, 0.0) for n in nmats]
    for mask in masks[1:]:
        n_s = [jnp.where(mask, n, 0.0) for n in nmats]
        ys = [n + _dot(q, n) for q, n in zip(qs, n_s)]
        yield
        qs = [q + y + _dot(y, q) for q, y in zip(qs, ys)]
        yield
    return qs


def _interleave(*stage_generators):
    live = list(stage_generators)
    while live:
        for gen in list(live):
            try:
                next(gen)
            except StopIteration:
                live.remove(gen)


def _mm_kernel(x_ref, g_ref, w_ref, o_ref, xn_ref, *, norm, act):
    @pl.when(pl.program_id(1) == 0)
    def _():
        x = x_ref[...].astype(F32)
        if norm:
            x = _rms(x, g_ref[...], RMS_EPS)
        xn_ref[...] = x.astype(BF16)

    y = jnp.dot(xn_ref[...], w_ref[...], preferred_element_type=F32)
    if act == "relu2":
        y = jnp.square(jnp.maximum(y, 0.0))
    o_ref[...] = y.astype(o_ref.dtype)


def _mm(x, w, g=None, act=None, out_dtype=F32, tm=1024, tn=1024):
    m, k = x.shape
    n = w.shape[1]
    tm = _tile(m, tm)
    tn = _tile(n, tn)
    norm = g is not None
    if g is None:
        g = jnp.ones((1, k), F32)
    return pl.pallas_call(
        functools.partial(_mm_kernel, norm=norm, act=act),
        out_shape=jax.ShapeDtypeStruct((m, n), out_dtype),
        grid=(m // tm, n // tn),
        in_specs=[pl.BlockSpec((tm, k), lambda i, j: (i, 0)),
                  pl.BlockSpec((1, k), lambda i, j: (0, 0)),
                  pl.BlockSpec((k, tn), lambda i, j: (0, j))],
        out_specs=pl.BlockSpec((tm, tn), lambda i, j: (i, j)),
        scratch_shapes=[pltpu.VMEM((tm, k), BF16)],
        compiler_params=_params("parallel", "arbitrary"),
        name="mm_norm",
    )(x, g, w)


def _mm_post_kernel(y_ref, w_ref, g_ref, res_ref, o_ref, acc_ref):
    kk = pl.program_id(1)

    @pl.when(kk == 0)
    def _():
        acc_ref[...] = jnp.zeros_like(acc_ref)

    acc_ref[...] += jnp.dot(y_ref[...].astype(BF16), w_ref[...], preferred_element_type=F32)

    @pl.when(kk == pl.num_programs(1) - 1)
    def _():
        o_ref[...] = res_ref[...] + _rms(acc_ref[...], g_ref[...], RMS_EPS)


def _mm_post(y, w, g, res, tm=512, tk=4096):
    m, k = y.shape
    n = w.shape[1]
    tm = _tile(m, tm)
    tk = _tile(k, tk)
    return pl.pallas_call(
        _mm_post_kernel,
        out_shape=jax.ShapeDtypeStruct((m, n), F32),
        grid=(m // tm, k // tk),
        in_specs=[pl.BlockSpec((tm, tk), lambda i, j: (i, j)),
                  pl.BlockSpec((tk, n), lambda i, j: (j, 0)),
                  pl.BlockSpec((1, n), lambda i, j: (0, 0)),
                  pl.BlockSpec((tm, n), lambda i, j: (i, 0))],
        out_specs=pl.BlockSpec((tm, n), lambda i, j: (i, 0)),
        scratch_shapes=[pltpu.VMEM((tm, n), F32)],
        compiler_params=_params("parallel", "arbitrary"),
        name="mm_post",
    )(y, w, g, res)


def _merge_kernel(ya_ref, yb_ref, yc_ref, pg_ref, x_ref, wa_ref, wb_ref, wc_ref, wo_ref, g_ref, o_ref):
    d = x_ref.shape[1]
    gates = jax.nn.sigmoid(pg_ref[...])
    merged = (gates[:, 0:d] * jnp.dot(ya_ref[...], wa_ref[...], preferred_element_type=F32)
              + gates[:, d:2 * d] * jnp.dot(yb_ref[...], wb_ref[...], preferred_element_type=F32)
              + gates[:, 2 * d:3 * d] * jnp.dot(yc_ref[...], wc_ref[...], preferred_element_type=F32))
    out = jnp.dot(merged.astype(BF16), wo_ref[...], preferred_element_type=F32)
    o_ref[...] = x_ref[...] + _rms(out, g_ref[...], RMS_EPS)


def _merge(ya, yb, yc, pg, x, wa, wb, wc, wo, g, tm=512):
    m, d = x.shape
    tm = _tile(m, tm)
    row = lambda i: (i, 0)
    full = lambda i: (0, 0)
    return pl.pallas_call(
        _merge_kernel,
        out_shape=jax.ShapeDtypeStruct((m, d), F32),
        grid=(m // tm,),
        in_specs=[pl.BlockSpec((tm, ya.shape[1]), row), pl.BlockSpec((tm, yb.shape[1]), row),
                  pl.BlockSpec((tm, yc.shape[1]), row), pl.BlockSpec((tm, 3 * d), row),
                  pl.BlockSpec((tm, d), row),
                  pl.BlockSpec(wa.shape, full), pl.BlockSpec(wb.shape, full), pl.BlockSpec(wc.shape, full),
                  pl.BlockSpec(wo.shape, full), pl.BlockSpec((1, d), full)],
        out_specs=pl.BlockSpec((tm, d), row),
        compiler_params=_params("parallel"),
        name="merge",
    )(ya, yb, yc, pg, x, wa, wb, wc, wo, g)


def _rwkv_stages(pa_ref, sh_ref, s0_ref, mu_ref, w0_ref, w2_ref, a0_ref, a2_ref, g2_ref, kk_ref, ka_ref, rk_ref,
                 lnw_ref, lnb_ref, y_ref, sout_ref, state_ref, carry_ref, *, chunk, t_valid, n_heads, hd):
    c = pl.program_id(1)
    width = n_heads * hd

    @pl.when(c == 0)
    def _():
        state_ref[...] = s0_ref[0]
        carry_ref[...] = sh_ref[0]

    pa = pa_ref[0]
    row = lax.broadcasted_iota(jnp.int32, (chunk, 1), 0)
    prev = jnp.where(row == 0, carry_ref[...], pltpu.roll(pa, 1, axis=0))
    carry_ref[...] = pa[chunk - 1:chunk, :]
    xa = pa + (prev - pa) * mu_ref[...]
    ar = xa[:, 0:width]
    ak = xa[:, width:2 * width]
    av = xa[:, 2 * width:3 * width]
    o = 3 * width
    aw = xa[:, o:o + LORA_W]
    aa = xa[:, o + LORA_W:o + LORA_W + LORA_A]
    ag = xa[:, o + LORA_W + LORA_A:o + LORA_W + LORA_A + LORA_G]

    w_raw = w0_ref[...] + _dot(jnp.tanh(aw), w2_ref[...])
    log_w = -jnp.exp(-jax.nn.softplus(-w_raw) - 0.5)
    a_in = jax.nn.sigmoid(a0_ref[...] + _dot(aa, a2_ref[...]))
    g_out = _dot(jax.nn.sigmoid(ag), g2_ref[...])
    kk_all = ak * kk_ref[...]
    k_all = ak * (1.0 + (a_in - 1.0) * ka_ref[...])
    rk_all = ar * k_all * rk_ref[...]

    masked = t_valid % chunk != 0
    valid = (row + c * chunk) < t_valid
    tri, strict = _tri_masks(chunk)
    pair_masks = _block_pair_masks(chunk)
    tri_f = tri.astype(F32)
    ln_w = lnw_ref[...]
    ln_b = lnb_ref[...]

    heads = range(n_heads)
    sls = [slice(h * hd, (h + 1) * hd) for h in heads]
    lo, hi = slice(0, chunk), slice(chunk, 2 * chunk)

    def prep(sl):
        kk = kk_all[:, sl]
        kk = kk * lax.rsqrt(jnp.sum(kk * kk, axis=-1, keepdims=True) + 1e-12)
        k = k_all[:, sl]
        b_vec = kk * a_in[:, sl]
        lw = log_w[:, sl]
        if masked:
            k = jnp.where(valid, k, 0.0)
            b_vec = jnp.where(valid, b_vec, 0.0)
            lw = jnp.where(valid, lw, 0.0)
        return -kk, b_vec, k, lw

    a_vecs, b_vecs, ks, lws = zip(*[prep(sl) for sl in sls])
    rs = [ar[:, sl] for sl in sls]
    vs = [av[:, sl] for sl in sls]
    yield
    g_incs = [_dot_f32(tri_f, lw) for lw in lws]
    yield
    g_lasts = [g[chunk - 1:chunk, :] for g in g_incs]
    x_mats, y_mats = [], []
    for h in heads:
        e_neg = jnp.exp(-g_incs[h])
        x_mats.append(jnp.concatenate([a_vecs[h] * jnp.exp(g_incs[h] - lws[h]), rs[h] * jnp.exp(g_incs[h])], axis=0))
        y_mats.append(jnp.concatenate([b_vecs[h] * e_neg, ks[h] * e_neg], axis=0))
    mms = [_dot_nt(x_mats[h], y_mats[h]) for h in heads]
    s0s = [state_ref[h] for h in heads]
    xss = [_dot_nt(x_mats[h], s0s[h]) for h in heads]
    yield
    rhss = [xss[h][lo] + _dot(jnp.where(strict, mms[h][lo, hi], 0.0), vs[h]) for h in heads]
    t_qs = yield from _unit_lower_inverse_minus_eye([jnp.where(strict, m[lo, lo], 0.0) for m in mms], pair_masks)
    us = [rhss[h] + _dot(t_qs[h], rhss[h]) for h in heads]
    yield
    ys = [xss[h][hi] + _dot(jnp.where(tri, mms[h][hi, lo], 0.0), us[h])
          + _dot(jnp.where(tri, mms[h][hi, hi], 0.0), vs[h]) for h in heads]
    yield
    for h in heads:
        e_tail = jnp.exp(g_lasts[h] - g_incs[h])
        state_ref[h] = (s0s[h] * jnp.exp(g_lasts[h])
                        + _dot_tn(jnp.concatenate([us[h], vs[h]], axis=0),
                                  jnp.concatenate([b_vecs[h] * e_tail, ks[h] * e_tail], axis=0)))

    outs = []
    for h in heads:
        y, sl = ys[h], sls[h]
        mean = jnp.mean(y, axis=-1, keepdims=True)
        var = jnp.mean(jnp.square(y - mean), axis=-1, keepdims=True)
        yn = (y - mean) * lax.rsqrt(var + GN_EPS_A) * ln_w[:, sl] + ln_b[:, sl]
        bonus = jnp.sum(rk_all[:, sl], axis=-1, keepdims=True) * vs[h]
        outs.append((yn + bonus) * g_out[:, sl])

    y_ref[0] = jnp.concatenate(outs, axis=-1).astype(y_ref.dtype)

    @pl.when(c == pl.num_programs(1) - 1)
    def _():
        sout_ref[0] = state_ref[...]


def _rwkv(pa3, a_shift, wkv0, lp, chunk):
    nb, t, cols = pa3.shape
    n_heads, hd = wkv0.shape[1], wkv0.shape[2]
    width = n_heads * hd
    tp = -(-t // chunk) * chunk
    if tp != t:
        pa3 = jnp.pad(pa3, ((0, 0), (0, tp - t), (0, 0)))
    full2 = lambda b, c: (0, 0)
    vec = lambda n: pl.BlockSpec((1, n), full2)
    return dict(
        stages=functools.partial(_rwkv_stages, chunk=chunk, t_valid=t, n_heads=n_heads, hd=hd),
        args=(pa3, a_shift.reshape(nb, 1, cols), wkv0, lp["a_mu"], lp["a_w0"], lp["a_w2"], lp["a_a0"], lp["a_a2"],
              lp["a_g2"], lp["a_kk"], lp["a_ka"], lp["a_rk"], lp["a_ln_w"], lp["a_ln_b"]),
        in_specs=[pl.BlockSpec((1, chunk, cols), lambda b, c: (b, c, 0)),
                  pl.BlockSpec((1, 1, cols), lambda b, c: (b, 0, 0)),
                  pl.BlockSpec((1, n_heads, hd, hd), lambda b, c: (b, 0, 0, 0)),
                  vec(cols), vec(width), pl.BlockSpec((LORA_W, width), full2),
                  vec(width), pl.BlockSpec((LORA_A, width), full2), pl.BlockSpec((LORA_G, width), full2),
                  vec(width), vec(width), vec(width), vec(width), vec(width)],
        out_shape=[jax.ShapeDtypeStruct((nb, tp, width), BF16), jax.ShapeDtypeStruct(wkv0.shape, F32)],
        out_specs=[pl.BlockSpec((1, chunk, width), lambda b, c: (b, c, 0)),
                   pl.BlockSpec((1, n_heads, hd, hd), lambda b, c: (b, 0, 0, 0))],
        scratch_shapes=[pltpu.VMEM((n_heads, hd, hd), F32), pltpu.VMEM((1, cols), F32)],
        grid=(nb, tp // chunk))


def _gdn_stages(pc_ref, ps_ref, cv_ref, s0_ref, cw_ref, alog_ref, dtb_ref, nw_ref,
                y_ref, sout_ref, state_ref, carry_ref, *, chunk, t_valid, n_heads, hd):
    c = pl.program_id(1)
    width = n_heads * hd
    taps = CONV_W - 1

    @pl.when(c == 0)
    def _():
        state_ref[...] = s0_ref[0]
        carry_ref[...] = cv_ref[0]

    pc = pc_ref[0]
    x = pc[:, 0:3 * width]
    cz = pc[:, 3 * width:4 * width]
    row = lax.broadcasted_iota(jnp.int32, (chunk, 1), 0)
    conv = x * cw_ref[taps:taps + 1, :]
    for j in range(1, CONV_W):
        shifted = pltpu.roll(x, j, axis=0)
        for i in range(j):
            shifted = jnp.where(row == i, carry_ref[taps - j + i:taps - j + i + 1, :], shifted)
        conv = conv + shifted * cw_ref[taps - j:taps - j + 1, :]
    carry_ref[...] = x[chunk - taps:chunk, :]
    act = jax.nn.silu(conv)
    q_all = act[:, 0:width]
    k_all = act[:, width:2 * width]
    v_all = act[:, 2 * width:3 * width]

    valid = (row + c * chunk) < t_valid
    ps = ps_ref[0]
    beta_all = jnp.where(valid, jax.nn.sigmoid(ps), 0.0)
    lane = lax.broadcasted_iota(jnp.int32, (1, ps.shape[1]), 1)
    dt_full = dtb_ref[...]
    neg_rate = -jnp.exp(alog_ref[...])
    g_all = jnp.where(valid & (lane >= n_heads) & (lane < 2 * n_heads),
                      neg_rate * jax.nn.softplus(ps + dt_full), 0.0)
    tri, strict = _tri_masks(chunk)
    pair_masks = _block_pair_masks(chunk)
    gc_all = _dot_f32(tri.astype(F32), g_all)
    yield
    gc_rows = gc_all.T

    heads = range(n_heads)
    sls = [slice(h * hd, (h + 1) * hd) for h in heads]
    lo, hi = slice(0, chunk), slice(chunk, 2 * chunk)
    qs, ks, vs, kbs, gcols, decays, e_gs, betas = [], [], [], [], [], [], [], []
    for h in heads:
        q = q_all[:, sls[h]]
        qs.append(q * lax.rsqrt(jnp.sum(q * q, axis=-1, keepdims=True) + 1e-6) * (hd ** -0.5))
        k = k_all[:, sls[h]]
        k = k * lax.rsqrt(jnp.sum(k * k, axis=-1, keepdims=True) + 1e-6)
        ks.append(k)
        vs.append(v_all[:, sls[h]])
        beta = beta_all[:, h:h + 1]
        betas.append(beta)
        kbs.append(k * beta)
        gcol = gc_all[:, n_heads + h:n_heads + h + 1]
        grow = gc_rows[n_heads + h:n_heads + h + 1, :]
        gcols.append(gcol)
        decays.append(jnp.where(tri, jnp.exp(jnp.where(tri, gcol - grow, 0.0)), 0.0))
        e_gs.append(jnp.exp(gcol))
    mms = [_dot_nt(jnp.concatenate([kbs[h], qs[h]], axis=0), ks[h]) for h in heads]
    yield
    t_qs = yield from _unit_lower_inverse_minus_eye(
        [jnp.where(strict, -(mms[h][lo] * decays[h]), 0.0) for h in heads], pair_masks)
    rhss = [jnp.concatenate([vs[h] * betas[h], kbs[h] * e_gs[h]], axis=1) for h in heads]
    uws = [rhss[h] + _dot(t_qs[h], rhss[h]) for h in heads]
    yield
    s0s = [state_ref[h] for h in heads]
    wss = [_dot(jnp.concatenate([uws[h][:, hd:2 * hd], qs[h] * e_gs[h]], axis=0), s0s[h]) for h in heads]
    yield
    v_news = [uws[h][:, 0:hd] - wss[h][lo] for h in heads]
    os_ = [wss[h][hi] + _dot(mms[h][hi] * decays[h], v_news[h]) for h in heads]
    yield
    for h in heads:
        g_last = gcols[h][chunk - 1:chunk, :]
        state_ref[h] = s0s[h] * jnp.exp(g_last) + _dot_tn(ks[h] * jnp.exp(g_last - gcols[h]), v_news[h])

    norm_w = nw_ref[...]
    outs = []
    for h in heads:
        o = os_[h]
        on = o * lax.rsqrt(jnp.mean(o * o, axis=-1, keepdims=True) + RMS_EPS) * norm_w
        outs.append(on * jax.nn.silu(cz[:, sls[h]]))

    y_ref[0] = jnp.concatenate(outs, axis=-1).astype(y_ref.dtype)

    @pl.when(c == pl.num_programs(1) - 1)
    def _():
        sout_ref[0] = state_ref[...]


def _gdn(pc3, ps3, c_conv, delta0, lp, chunk):
    nb, t, cols = pc3.shape
    n_heads, hd = delta0.shape[1], delta0.shape[2]
    width = n_heads * hd
    taps = CONV_W - 1
    tp = -(-t // chunk) * chunk
    if tp != t:
        pc3 = jnp.pad(pc3, ((0, 0), (0, tp - t), (0, 0)))
        ps3 = jnp.pad(ps3, ((0, 0), (0, tp - t), (0, 0)))
    full2 = lambda b, c: (0, 0)
    return dict(
        stages=functools.partial(_gdn_stages, chunk=chunk, t_valid=t, n_heads=n_heads, hd=hd),
        args=(pc3, ps3, c_conv, delta0, lp["c_conv_w"], lp["c_a_log_pad"], lp["c_dt_bias_pad"], lp["c_norm_w"]),
        in_specs=[pl.BlockSpec((1, chunk, cols), lambda b, c: (b, c, 0)),
                  pl.BlockSpec((1, chunk, LANES), lambda b, c: (b, c, 0)),
                  pl.BlockSpec((1, taps, 3 * width), lambda b, c: (b, 0, 0)),
                  pl.BlockSpec((1, n_heads, hd, hd), lambda b, c: (b, 0, 0, 0)),
                  pl.BlockSpec((CONV_W, 3 * width), full2),
                  pl.BlockSpec((1, LANES), full2), pl.BlockSpec((1, LANES), full2),
                  pl.BlockSpec((1, hd), full2)],
        out_shape=[jax.ShapeDtypeStruct((nb, tp, width), BF16), jax.ShapeDtypeStruct(delta0.shape, F32)],
        out_specs=[pl.BlockSpec((1, chunk, width), lambda b, c: (b, c, 0)),
                   pl.BlockSpec((1, n_heads, hd, hd), lambda b, c: (b, 0, 0, 0))],
        scratch_shapes=[pltpu.VMEM((n_heads, hd, hd), F32), pltpu.VMEM((taps, 3 * width), F32)],
        grid=(nb, tp // chunk))


def _recurrent_kernel(*refs, parts):
    groups = []
    pos = 0
    for key in ("n_in", "n_out", "n_scratch"):
        group = []
        for part in parts:
            group.append(refs[pos:pos + part[key]])
            pos += part[key]
        groups.append(group)
    ins, outs, scratch = groups
    _interleave(*[part["stages"](*ins[i], *outs[i], *scratch[i]) for i, part in enumerate(parts)])


def _recurrent_mixers(specs):
    grid = specs[0]["grid"]
    assert all(s["grid"] == grid for s in specs)
    parts = tuple(dict(stages=s["stages"], n_in=len(s["in_specs"]), n_out=len(s["out_specs"]),
                       n_scratch=len(s["scratch_shapes"])) for s in specs)
    flat = lambda key: [item for s in specs for item in s[key]]
    outs = pl.pallas_call(
        functools.partial(_recurrent_kernel, parts=parts),
        out_shape=flat("out_shape"),
        grid=grid,
        in_specs=flat("in_specs"),
        out_specs=flat("out_specs"),
        scratch_shapes=flat("scratch_shapes"),
        compiler_params=_params("parallel", "arbitrary"),
        name="recurrent_mixers",
    )(*flat("args"))
    results, pos = [], 0
    for s in specs:
        results.append(outs[pos:pos + len(s["out_specs"])])
        pos += len(s["out_specs"])
    return results


def _t5_bucket(dist):
    n = jnp.maximum(dist, 0)
    max_exact = N_BUCKETS // 2
    nf = jnp.maximum(n, 1).astype(F32)
    large = max_exact + (jnp.log(nf / max_exact) / math.log(MAX_DISTANCE / max_exact)
                         * (N_BUCKETS - max_exact)).astype(jnp.int32)
    return jnp.where(n < max_exact, n, jnp.minimum(large, N_BUCKETS - 1))


def _diff_attn_kernel(rb_ref, sc_ref, q_ref, k_ref, v_ref, sub_ref, o_ref, bias_ref, kb_ref, vb_ref, *, blk, hd):
    h = pl.program_id(0)
    qi = pl.program_id(2)
    far_bias = rb_ref[N_BUCKETS - 1, h]

    @pl.when(qi == 0)
    def _():
        kb_ref[...] = k_ref[0].astype(BF16)
        vb_ref[...] = v_ref[0].astype(BF16)

    @pl.when((qi == 0) & (pl.program_id(1) == 0))
    def _():
        r = lax.broadcasted_iota(jnp.int32, (blk, blk), 0)
        c = lax.broadcasted_iota(jnp.int32, (blk, blk), 1)
        for delta in range(2):
            dist = delta * blk + r - c
            bucket = _t5_bucket(dist)
            bias = jnp.zeros((blk, blk), F32)
            for n in range(N_BUCKETS):
                bias = jnp.where(bucket == n, rb_ref[n, h], bias)
            bias_ref[delta] = jnp.where(dist >= 0, bias, NEG_INF)

    q = q_ref[0] * (hd ** -0.5)
    first_half = lax.broadcasted_iota(jnp.int32, (1, 2 * hd), 1) < hd
    qs = (jnp.where(first_half, q, 0.0).astype(BF16), jnp.where(first_half, 0.0, q).astype(BF16))

    def step(kj, bias, carry):
        start = pl.multiple_of(kj * blk, blk)
        kt = kb_ref[pl.ds(start, blk), :]
        vt = vb_ref[pl.ds(start, blk), :]
        new = []
        for comp in range(2):
            m, l, acc = carry[comp]
            s = _dot_nt(qs[comp], kt) + bias
            m_new = jnp.maximum(m, jnp.max(s, axis=-1, keepdims=True))
            alpha = jnp.exp(m - m_new)
            p = jnp.exp(s - m_new)
            l_new = alpha * l + jnp.sum(p, axis=-1, keepdims=True)
            acc_new = alpha * acc + jnp.dot(p.astype(BF16), vt, preferred_element_type=F32)
            new.append((m_new, l_new, acc_new))
        return tuple(new)

    init = tuple((jnp.full((blk, 1), NEG_INF, F32), jnp.zeros((blk, 1), F32), jnp.zeros((blk, 2 * hd), F32))
                 for _ in range(2))
    carry = lax.fori_loop(0, jnp.maximum(qi - 1, 0), lambda kj, cr: step(kj, far_bias, cr), init)
    carry = lax.cond(qi >= 1, lambda cr: step(qi - 1, bias_ref[1], cr), lambda cr: cr, carry)
    carry = step(qi, bias_ref[0], carry)

    (_, l1, acc1), (_, l2, acc2) = carry
    out = acc1 / l1 - sc_ref[0] * (acc2 / l2)
    out = _rms(out, sub_ref[...], SUBLN_EPS) * sc_ref[1]
    o_ref[0] = out.astype(o_ref.dtype)


def _diff_attn_prompt(pb3, rel_bias, scal, subln):
    nb, t, cols = pb3.shape
    n_heads = rel_bias.shape[1]
    hw = cols // (3 * n_heads)
    blk = next(b for b in (ATT_BLOCK, ATT_BLOCK // 2, MAX_DISTANCE) if t % b == 0)
    assert t % blk == 0 and MAX_DISTANCE <= blk
    smem = pl.BlockSpec(memory_space=pltpu.SMEM)
    return pl.pallas_call(
        functools.partial(_diff_attn_kernel, blk=blk, hd=hw // 2),
        out_shape=jax.ShapeDtypeStruct((nb, t, n_heads * hw), BF16),
        grid=(n_heads, nb, t // blk),
        in_specs=[smem, smem,
                  pl.BlockSpec((1, blk, hw), lambda h, b, i: (b, i, h)),
                  pl.BlockSpec((1, t, hw), lambda h, b, i: (b, 0, n_heads + h)),
                  pl.BlockSpec((1, t, hw), lambda h, b, i: (b, 0, 2 * n_heads + h)),
                  pl.BlockSpec((1, hw), lambda h, b, i: (0, 0))],
        out_specs=pl.BlockSpec((1, blk, hw), lambda h, b, i: (b, i, h)),
        scratch_shapes=[pltpu.VMEM((2, blk, blk), F32), pltpu.VMEM((t, hw), BF16), pltpu.VMEM((t, hw), BF16)],
        compiler_params=_params("arbitrary", "arbitrary", "arbitrary"),
        name="diff_attn_prompt",
    )(rel_bias, scal, pb3, pb3, pb3, subln)


def _paged_attn_kernel(pt_ref, sc_ref, q_ref, kn_ref, vn_ref, rb_ref, sub_ref, *rest, n_heads, hd, page, n_pages):
    k_refs = rest[0:n_pages]
    v_refs = rest[n_pages:2 * n_pages]
    o_ref, bias_ref = rest[2 * n_pages], rest[2 * n_pages + 1]
    n_maps = 2 * n_heads
    rows = page * n_heads
    map_idx = lax.broadcasted_iota(jnp.int32, (n_maps, 1), 0)

    @pl.when(pl.program_id(0) == 0)
    def _():
        lane = lax.broadcasted_iota(jnp.int32, (1, rows), 1)
        own_head = (lane % n_heads) == (map_idx // 2)
        far = jnp.broadcast_to(rb_ref[:, N_BUCKETS - 1:N_BUCKETS], (n_maps, rows))
        bucket = _t5_bucket(page - lane // n_heads)
        near = jnp.zeros((n_maps, rows), F32)
        for n in range(N_BUCKETS):
            near = jnp.where(bucket == n, rb_ref[:, n:n + 1], near)
        bias_ref[0] = jnp.where(own_head, far, NEG_INF)
        bias_ref[1] = jnp.where(own_head, near, NEG_INF)

    own_half = (lax.broadcasted_iota(jnp.int32, (1, 2 * hd), 1) // hd) == (map_idx % 2)
    qm = jnp.where(own_half, q_ref[0] * (hd ** -0.5), 0.0)
    qb = qm.astype(BF16)

    scores = [_dot_nt(qb, k_refs[i][0, 0].astype(BF16)) + bias_ref[1 if i == n_pages - 1 else 0]
              for i in range(n_pages)]
    s_self = jnp.sum(qm * kn_ref[0], axis=1, keepdims=True) + rb_ref[:, 0:1]
    m = s_self
    for s in scores:
        m = jnp.maximum(m, jnp.max(s, axis=1, keepdims=True))
    p_self = jnp.exp(s_self - m)
    l = p_self
    acc = p_self * vn_ref[0]
    for i in range(n_pages):
        p = jnp.exp(scores[i] - m)
        l = l + jnp.sum(p, axis=1, keepdims=True)
        acc = acc + jnp.dot(p.astype(BF16), v_refs[i][0, 0].astype(BF16), preferred_element_type=F32)
    o = acc / l
    d = o - sc_ref[0] * pltpu.roll(o, n_maps - 1, axis=0)
    o_ref[0] = (_rms(d, sub_ref[...], SUBLN_EPS) * sc_ref[1]).astype(o_ref.dtype)


def _diff_attn_sample(pb, cache_k, cache_v, page_table, layer, rel_bias, scal, subln):
    nbs, cols = pb.shape
    n_heads = rel_bias.shape[1]
    width = cols // 3
    hw = width // n_heads
    n_pool, depth, page = cache_k.shape[0], cache_k.shape[1], cache_k.shape[2]
    n_pages = page_table.shape[1]
    assert MAX_DISTANCE <= page
    n_maps = 2 * n_heads
    rows = page * n_heads
    ck = cache_k.reshape(n_pool, depth, rows, hw)
    cv = cache_v.reshape(n_pool, depth, rows, hw)
    qkv = jnp.repeat(pb.reshape(nbs, 3, n_heads, hw), 2, axis=2)
    rb_maps = jnp.repeat(rel_bias, 2, axis=1).T
    per_seq = pl.BlockSpec((1, n_maps, hw), lambda b, pt: (b, 0, 0))
    page_spec = lambda i: pl.BlockSpec((1, 1, rows, hw), lambda b, pt: (pt[b * n_pages + i], layer, 0, 0))
    grid_spec = pltpu.PrefetchScalarGridSpec(
        num_scalar_prefetch=1,
        grid=(nbs,),
        in_specs=[pl.BlockSpec(memory_space=pltpu.SMEM), per_seq, per_seq, per_seq,
                  pl.BlockSpec(rb_maps.shape, lambda b, pt: (0, 0)),
                  pl.BlockSpec((1, hw), lambda b, pt: (0, 0))]
                 + [page_spec(i) for i in range(n_pages)] + [page_spec(i) for i in range(n_pages)],
        out_specs=per_seq,
        scratch_shapes=[pltpu.VMEM((2, n_maps, rows), F32)])
    out = pl.pallas_call(
        functools.partial(_paged_attn_kernel, n_heads=n_heads, hd=hw // 2, page=page, n_pages=n_pages),
        out_shape=jax.ShapeDtypeStruct((nbs, n_maps, hw), BF16),
        grid_spec=grid_spec,
        compiler_params=_params("arbitrary"),
        name="diff_attn_paged",
    )(page_table.reshape(-1), scal, qkv[:, 0], qkv[:, 1], qkv[:, 2], rb_maps, subln,
      *([ck] * n_pages), *([cv] * n_pages))
    return out[:, 0::2].reshape(nbs, width)


def _mem_attn_kernel(q_ref, k_ref, v_ref, o_ref, *, n_heads, hd):
    q = q_ref[0] * (hd ** -0.5)
    k = k_ref[0]
    v = v_ref[0]
    sls = [slice(h * hd, (h + 1) * hd) for h in range(n_heads)]
    scores = [_dot_nt(q[:, sl], k[:, sl]) for sl in sls]
    probs = []
    for s in scores:
        p = jnp.exp(s - jnp.max(s, axis=-1, keepdims=True))
        probs.append(p / jnp.sum(p, axis=-1, keepdims=True))
    outs = [_dot(a, v[:, sl]) for a, sl in zip(probs, sls)]
    o_ref[0] = jnp.concatenate(outs, axis=-1).astype(o_ref.dtype)


def _mem_attn(q3, mk, mv, n_heads, tq=512):
    nb, t, w = q3.shape
    nm = mk.shape[1]
    tq = _tile(t, tq)
    return pl.pallas_call(
        functools.partial(_mem_attn_kernel, n_heads=n_heads, hd=w // n_heads),
        out_shape=jax.ShapeDtypeStruct((nb, t, w), BF16),
        grid=(nb, t // tq),
        in_specs=[pl.BlockSpec((1, tq, w), lambda b, i: (b, i, 0)),
                  pl.BlockSpec((1, nm, w), lambda b, i: (b, 0, 0)),
                  pl.BlockSpec((1, nm, w), lambda b, i: (b, 0, 0))],
        out_specs=pl.BlockSpec((1, tq, w), lambda b, i: (b, i, 0)),
        compiler_params=_params("parallel", "arbitrary"),
        name="mem_attn",
    )(q3, mk, mv)


def _layer(lp, layer, x, mem_k, mem_v, a_shift, a_wkv, c_conv, c_delta, paged):
    nb, t, d = x.shape
    m = nb * t
    x2 = x.reshape(m, d)
    g = lp["g_pre_mix"]
    pa = _mm(x2, lp["w_in_a"], g, tn=896)
    pb = _mm(x2, lp["w_in_b"], g)
    pc = _mm(x2, lp["w_in_c"], g)
    ps = _mm(x2, lp["w_in_s"], g)
    pg = _mm(x2, lp["w_in_g"], g)

    chunk = CHUNK if t >= CHUNK else SUBLANES
    pa3 = pa.reshape(nb, t, -1)
    pc3 = pc.reshape(nb, t, -1)
    (ya, new_wkv), (yc, new_delta) = _recurrent_mixers(
        [_rwkv(pa3, a_shift, a_wkv, lp, chunk), _gdn(pc3, ps.reshape(nb, t, -1), c_conv, c_delta, lp, chunk)])
    ya, yc = ya[:, :t], yc[:, :t]
    new_shift = pa3[:, t - 1]

    n_heads_b = lp["rel_bias"].shape[1]
    w_b = pb.shape[1] // 3
    if paged is None:
        yb = _diff_attn_prompt(pb.reshape(nb, t, -1), lp["rel_bias"], lp["b_scal"], lp["b_subln"])
    else:
        cache_k, cache_v, page_table = paged
        yb = _diff_attn_sample(pb, cache_k, cache_v, page_table, layer, lp["rel_bias"], lp["b_scal"], lp["b_subln"])
    new_bk = pb[:, w_b:2 * w_b].reshape(nb, t, n_heads_b, -1)
    new_bv = pb[:, 2 * w_b:3 * w_b].reshape(nb, t, n_heads_b, -1)

    w_c3 = c_conv.shape[2]
    new_conv = jnp.concatenate([c_conv, pc3[:, :, 0:w_c3]], axis=1)[:, t:]

    x2 = _merge(ya.reshape(m, -1), yb.reshape(m, -1), yc.reshape(m, -1), pg, x2,
                lp["w_br_a"], lp["w_br_b"], lp["w_br_c"], lp["w_out"], lp["g_post_mix"])

    n_heads_m = mem_k.shape[2]
    q = _mm(x2, lp["w_mem_q"], lp["g_pre_mem"])
    att = _mem_attn(q.reshape(nb, t, -1), mem_k.reshape(nb, mem_k.shape[1], -1),
                    mem_v.reshape(nb, mem_v.shape[1], -1), n_heads_m)
    x2 = _mm_post(att.reshape(m, -1), lp["w_mem_o"], lp["g_post_mem"], x2)

    hf = _mm(x2, lp["w_ffn1"], lp["g_pre_ffn"], act="relu2", out_dtype=BF16)
    x2 = _mm_post(hf, lp["w_ffn2"], lp["g_post_ffn"], x2)
    return x2.reshape(nb, t, d), (new_shift, new_wkv, new_conv, new_delta, new_bk, new_bv)


def _layer_params(p, l):
    d = p["w_in"].shape[1]
    n_ha, hd_a = p["a_rk"].shape[1], p["a_rk"].shape[2]
    w_a = n_ha * hd_a
    a_cols = 3 * w_a + LORA_W + LORA_A + LORA_G
    w_b = p["w_br_b"].shape[1]
    w_c = p["w_br_c"].shape[1]
    n_hc = p["c_a_log"].shape[1]
    b_cols = 3 * w_b
    row = lambda v: v.reshape(1, -1).astype(F32)
    w_in = p["w_in"][l]
    o_b = a_cols
    o_c = o_b + b_cols
    o_s = o_c + 4 * w_c
    o_g = o_s + 2 * n_hc
    lam_init = 0.8 - 0.6 * math.exp(-0.3 * l)
    lam = (jnp.exp(jnp.sum(p["b_lam_q1"][l] * p["b_lam_k1"][l]))
           - jnp.exp(jnp.sum(p["b_lam_q2"][l] * p["b_lam_k2"][l])) + lam_init)
    pad_heads = lambda v: jnp.zeros((1, LANES), F32).at[0, n_hc:2 * n_hc].set(v)
    lp = {
        "w_in_a": w_in[:, 0:o_b].astype(BF16),
        "w_in_b": w_in[:, o_b:o_c].astype(BF16),
        "w_in_c": w_in[:, o_c:o_s].astype(BF16),
        "w_in_s": jnp.pad(w_in[:, o_s:o_g], ((0, 0), (0, LANES - 2 * n_hc))).astype(BF16),
        "w_in_g": w_in[:, o_g:o_g + 3 * d].astype(BF16),
        "a_mu": row(p["a_mu"][l]), "a_w0": row(p["a_w0"][l]), "a_w2": p["a_w2"][l].astype(BF16),
        "a_a0": row(p["a_a0"][l]), "a_a2": p["a_a2"][l].astype(BF16), "a_g2": p["a_g2"][l].astype(BF16),
        "a_kk": row(p["a_kk"][l]), "a_ka": row(p["a_ka"][l]), "a_rk": row(p["a_rk"][l]),
        "a_ln_w": row(p["a_ln_w"][l]), "a_ln_b": row(p["a_ln_b"][l]),
        "rel_bias": p["rel_bias"].astype(F32),
        "b_scal": jnp.stack([lam, jnp.asarray(1.0 - lam_init, F32)]).astype(F32),
        "b_subln": row(p["b_subln"][l]),
        "c_conv_w": p["c_conv_w"][l].astype(F32),
        "c_a_log_pad": pad_heads(p["c_a_log"][l]), "c_dt_bias_pad": pad_heads(p["c_dt_bias"][l]),
        "c_norm_w": row(p["c_norm_w"][l]),
        "w_br_a": p["w_br_a"][l].astype(BF16), "w_br_b": p["w_br_b"][l].astype(BF16),
        "w_br_c": p["w_br_c"][l].astype(BF16), "w_out": p["w_out"][l].astype(BF16),
        "g_mem": row(p["g_mem"][l]), "w_mem_q": p["w_mem_q"][l].astype(BF16),
        "w_mem_kv": p["w_mem_kv"][l].astype(BF16), "w_mem_o": p["w_mem_o"][l].astype(BF16),
        "w_ffn1": p["w_ffn1"][l].astype(BF16), "w_ffn2": p["w_ffn2"][l].astype(BF16),
    }
    for name in ("g_pre_mix", "g_post_mix", "g_pre_mem", "g_post_mem", "g_pre_ffn", "g_post_ffn"):
        lp[name] = row(p[name][l])
    return lp


def kernel(x_prompt, x_sample, mem_prompt, cache_b_k, cache_b_v, page_table, state_a_wkv, state_a_shift, state_c_delta, state_c_conv, cache_mem_k, cache_mem_v, g_pre_mix, g_post_mix, g_pre_mem, g_post_mem, g_pre_ffn, g_post_ffn, w_in, a_mu, a_w0, a_w2, a_a0, a_a2, a_g2, a_kk, a_ka, a_rk, a_ln_w, a_ln_b, b_lam_q1, b_lam_k1, b_lam_q2, b_lam_k2, b_subln, rel_bias, c_conv_w, c_a_log, c_dt_bias, c_norm_w, w_br_a, w_br_b, w_br_c, w_out, g_mem, w_mem_q, w_mem_kv, w_mem_o, w_ffn1, w_ffn2):
    p = dict(g_pre_mix=g_pre_mix, g_post_mix=g_post_mix, g_pre_mem=g_pre_mem, g_post_mem=g_post_mem,
             g_pre_ffn=g_pre_ffn, g_post_ffn=g_post_ffn, w_in=w_in,
             a_mu=a_mu, a_w0=a_w0, a_w2=a_w2, a_a0=a_a0, a_a2=a_a2, a_g2=a_g2, a_kk=a_kk, a_ka=a_ka,
             a_rk=a_rk, a_ln_w=a_ln_w, a_ln_b=a_ln_b,
             b_lam_q1=b_lam_q1, b_lam_k1=b_lam_k1, b_lam_q2=b_lam_q2, b_lam_k2=b_lam_k2,
             b_subln=b_subln, rel_bias=rel_bias,
             c_conv_w=c_conv_w, c_a_log=c_a_log, c_dt_bias=c_dt_bias, c_norm_w=c_norm_w,
             w_br_a=w_br_a, w_br_b=w_br_b, w_br_c=w_br_c, w_out=w_out,
             g_mem=g_mem, w_mem_q=w_mem_q, w_mem_kv=w_mem_kv, w_mem_o=w_mem_o,
             w_ffn1=w_ffn1, w_ffn2=w_ffn2)
    depth = w_in.shape[0]
    nbp = x_prompt.shape[0]
    n_ha, hd_a = a_rk.shape[1], a_rk.shape[2]
    a_cols = a_mu.shape[1]
    n_hc = c_a_log.shape[1]
    hd_c = c_norm_w.shape[1]
    n_hm = cache_mem_k.shape[3]
    n_mem = mem_prompt.shape[1]
    taps = c_conv_w.shape[1] - 1

    xp, xs = x_prompt, x_sample
    outs_p, outs_s, mem_kv = [], [], []
    for l in range(depth):
        lp = _layer_params(p, l)
        mkv = _mm(mem_prompt.reshape(nbp * n_mem, -1), lp["w_mem_kv"], lp["g_mem"])
        w_mem = mkv.shape[1] // 2
        mk = mkv[:, 0:w_mem].reshape(nbp, n_mem, n_hm, -1)
        mv = mkv[:, w_mem:2 * w_mem].reshape(nbp, n_mem, n_hm, -1)
        mem_kv.append((mk, mv))
        xp, st = _layer(lp, l, xp, mk, mv,
                        jnp.zeros((nbp, a_cols), F32), jnp.zeros((nbp, n_ha, hd_a, hd_a), F32),
                        jnp.zeros((nbp, taps, 3 * n_hc * hd_c), F32), jnp.zeros((nbp, n_hc, hd_c, hd_c), F32),
                        None)
        outs_p.append(st)
        xs, st = _layer(lp, l, xs, cache_mem_k[l], cache_mem_v[l],
                        state_a_shift[l], state_a_wkv[l], state_c_conv[l], state_c_delta[l],
                        (cache_b_k, cache_b_v, page_table))
        outs_s.append(st)

    stack = lambda group, i, axis: jnp.stack([st[i] for st in group], axis=axis)
    return (xp, xs,
            stack(outs_p, 4, 1), stack(outs_p, 5, 1), stack(outs_s, 4, 1), stack(outs_s, 5, 1),
            stack(outs_p, 1, 0), stack(outs_s, 1, 0), stack(outs_p, 0, 0), stack(outs_s, 0, 0),
            stack(outs_p, 3, 0), stack(outs_s, 3, 0), stack(outs_p, 2, 0), stack(outs_s, 2, 0),
            jnp.stack([m[0] for m in mem_kv], axis=0), jnp.stack([m[1] for m in mem_kv], axis=0))
```

```python
import functools
import math

import jax
import jax.numpy as jnp
from jax import lax
from jax.experimental import pallas as pl
from jax.experimental.pallas import tpu as pltpu

F32 = jnp.float32
BF16 = jnp.bfloat16

RMS_EPS = 1e-6
GN_EPS_A = 64e-5
SUBLN_EPS = 1e-5
N_BUCKETS = 32
MAX_DISTANCE = 128
CONV_W = 4
LORA_W = 64
LORA_A = 64
LORA_G = 128

LANES = 128
SUBLANES = 8
VMEM_LIMIT_BYTES = 56 * 1024 * 1024

CHUNK = 64
SEQS_PER_STEP_LONG = 1
SEQS_PER_STEP_SHORT = 4
ATT_BLOCK = 512
NEG_INF = -1e30


def _params(*sem):
    return pltpu.CompilerParams(dimension_semantics=sem, vmem_limit_bytes=VMEM_LIMIT_BYTES)


def _tile(n, pref):
    if n <= pref:
        return n
    for t in range(pref, 7, -1):
        if n % t == 0 and t % SUBLANES == 0:
            return t
    return n


def _dot(a, b):
    return jnp.dot(a.astype(BF16), b.astype(BF16), preferred_element_type=F32)


def _dot_nt(a, b):
    return lax.dot_general(a.astype(BF16), b.astype(BF16), (((1,), (1,)), ((), ())), preferred_element_type=F32)


def _dot_tn(a, b):
    return lax.dot_general(a.astype(BF16), b.astype(BF16), (((0,), (0,)), ((), ())), preferred_element_type=F32)


def _dot_f32(a, b):
    return jnp.dot(a, b, preferred_element_type=F32, precision=lax.Precision.HIGHEST)


def _rms(x, g, eps):
    return x * lax.rsqrt(jnp.mean(x * x, axis=-1, keepdims=True) + eps) * g


def _tri_masks(n):
    r = lax.broadcasted_iota(jnp.int32, (n, n), 0)
    c = lax.broadcasted_iota(jnp.int32, (n, n), 1)
    return r >= c, r > c


def _block_pair_masks(n):
    r = lax.broadcasted_iota(jnp.int32, (n, n), 0)
    c = lax.broadcasted_iota(jnp.int32, (n, n), 1)
    masks = []
    s = 1
    while s < n:
        masks.append(((r // (2 * s)) == (c // (2 * s))) & ((r // s) % 2 == 1) & ((c // s) % 2 == 0))
        s *= 2
    return masks


def _unit_lower_inverse_minus_eye(nmats, masks):
    qs = [jnp.where(masks[0], n, 0.0) for n in nmats]
    for mask in masks[1:]:
        n_s = [jnp.where(mask, n, 0.0) for n in nmats]
        ys = [n + _dot(q, n) for q, n in zip(qs, n_s)]
        yield
        qs = [q + y + _dot(y, q) for q, y in zip(qs, ys)]
        yield
    return qs


def _interleave(*stage_generators):
    live = list(stage_generators)
    while live:
        for gen in list(live):
            try:
                next(gen)
            except StopIteration:
                live.remove(gen)


def _mm_kernel(x_ref, g_ref, w_ref, o_ref, xn_ref, *, norm, act):
    @pl.when(pl.program_id(1) == 0)
    def _():
        x = x_ref[...].astype(F32)
        if norm:
            x = _rms(x, g_ref[...], RMS_EPS)
        xn_ref[...] = x.astype(BF16)

    y = jnp.dot(xn_ref[...], w_ref[...], preferred_element_type=F32)
    if act == "relu2":
        y = jnp.square(jnp.maximum(y, 0.0))
    o_ref[...] = y.astype(o_ref.dtype)


def _mm(x, w, g=None, act=None, out_dtype=F32, tm=1024, tn=1024):
    m, k = x.shape
    n = w.shape[1]
    tm = _tile(m, tm)
    tn = _tile(n, tn)
    norm = g is not None
    if g is None:
        g = jnp.ones((1, k), F32)
    return pl.pallas_call(
        functools.partial(_mm_kernel, norm=norm, act=act),
        out_shape=jax.ShapeDtypeStruct((m, n), out_dtype),
        grid=(m // tm, n // tn),
        in_specs=[pl.BlockSpec((tm, k), lambda i, j: (i, 0)),
                  pl.BlockSpec((1, k), lambda i, j: (0, 0)),
                  pl.BlockSpec((k, tn), lambda i, j: (0, j))],
        out_specs=pl.BlockSpec((tm, tn), lambda i, j: (i, j)),
        scratch_shapes=[pltpu.VMEM((tm, k), BF16)],
        compiler_params=_params("parallel", "arbitrary"),
        name="mm_norm",
    )(x, g, w)


def _mm_qkv_kernel(x_ref, g_ref, w_ref, k_in, v_in, o_ref, k_out, v_out, xn_ref, *, n_heads, rows_are_tokens):
    del k_in, v_in
    j = pl.program_id(1)

    @pl.when(j == 0)
    def _():
        xn_ref[...] = _rms(x_ref[...].astype(F32), g_ref[...], RMS_EPS).astype(BF16)

    y = jnp.dot(xn_ref[...], w_ref[...], preferred_element_type=F32)
    o_ref[...] = y
    hw = y.shape[1] // n_heads

    def split_heads(dst):
        for h in range(n_heads):
            if rows_are_tokens:
                dst[0, 0, :, h, :] = y[:, h * hw:(h + 1) * hw]
            else:
                dst[:, 0, 0, h, :] = y[:, h * hw:(h + 1) * hw]

    @pl.when(j == 1)
    def _():
        split_heads(k_out)

    @pl.when(j == 2)
    def _():
        split_heads(v_out)


def _mm_qkv(x, w, g, nb, t, layer, k_stack, v_stack, tm=1024):
    m, k = x.shape
    tn = w.shape[1] // 3
    _, depth, _, n_heads, hw = k_stack.shape
    assert n_heads * hw == tn
    tm = _tile(t, tm) if t > 1 else _tile(m, tm)
    if t > 1:
        per_seq = t // tm
        kv_spec = pl.BlockSpec((1, 1, tm, n_heads, hw), lambda i, j: (i // per_seq, layer, i % per_seq, 0, 0))
    else:
        kv_spec = pl.BlockSpec((tm, 1, 1, n_heads, hw), lambda i, j: (i, layer, 0, 0, 0))
    any_spec = pl.BlockSpec(memory_space=pl.ANY)
    return pl.pallas_call(
        functools.partial(_mm_qkv_kernel, n_heads=n_heads, rows_are_tokens=t > 1),
        out_shape=(jax.ShapeDtypeStruct((m, 3 * tn), F32),
                   jax.ShapeDtypeStruct(k_stack.shape, F32), jax.ShapeDtypeStruct(v_stack.shape, F32)),
        grid=(m // tm, 3),
        in_specs=[pl.BlockSpec((tm, k), lambda i, j: (i, 0)),
                  pl.BlockSpec((1, k), lambda i, j: (0, 0)),
                  pl.BlockSpec((k, tn), lambda i, j: (0, j)),
                  any_spec, any_spec],
        out_specs=(pl.BlockSpec((tm, tn), lambda i, j: (i, j)), kv_spec, kv_spec),
        scratch_shapes=[pltpu.VMEM((tm, k), BF16)],
        input_output_aliases={3: 1, 4: 2},
        compiler_params=_params("parallel", "arbitrary"),
        name="mm_qkv",
    )(x, g, w, k_stack, v_stack)


def _mm_post_kernel(y_ref, w_ref, g_ref, res_ref, o_ref, acc_ref):
    kk = pl.program_id(1)

    @pl.when(kk == 0)
    def _():
        acc_ref[...] = jnp.zeros_like(acc_ref)

    acc_ref[...] += jnp.dot(y_ref[...].astype(BF16), w_ref[...], preferred_element_type=F32)

    @pl.when(kk == pl.num_programs(1) - 1)
    def _():
        o_ref[...] = res_ref[...] + _rms(acc_ref[...], g_ref[...], RMS_EPS)


def _mm_post(y, w, g, res, tm=512, tk=4096):
    m, k = y.shape
    n = w.shape[1]
    tm = _tile(m, tm)
    tk = _tile(k, tk)
    return pl.pallas_call(
        _mm_post_kernel,
        out_shape=jax.ShapeDtypeStruct((m, n), F32),
        grid=(m // tm, k // tk),
        in_specs=[pl.BlockSpec((tm, tk), lambda i, j: (i, j)),
                  pl.BlockSpec((tk, n), lambda i, j: (j, 0)),
                  pl.BlockSpec((1, n), lambda i, j: (0, 0)),
                  pl.BlockSpec((tm, n), lambda i, j: (i, 0))],
        out_specs=pl.BlockSpec((tm, n), lambda i, j: (i, 0)),
        scratch_shapes=[pltpu.VMEM((tm, n), F32)],
        compiler_params=_params("parallel", "arbitrary"),
        name="mm_post",
    )(y, w, g, res)


def _merge_kernel(ya_ref, yb_ref, yc_ref, pg_ref, x_ref, wa_ref, wb_ref, wc_ref, wo_ref, g_ref, o_ref):
    d = x_ref.shape[1]
    gates = jax.nn.sigmoid(pg_ref[...])
    merged = (gates[:, 0:d] * jnp.dot(ya_ref[...], wa_ref[...], preferred_element_type=F32)
              + gates[:, d:2 * d] * jnp.dot(yb_ref[...], wb_ref[...], preferred_element_type=F32)
              + gates[:, 2 * d:3 * d] * jnp.dot(yc_ref[...], wc_ref[...], preferred_element_type=F32))
    out = jnp.dot(merged.astype(BF16), wo_ref[...], preferred_element_type=F32)
    o_ref[...] = x_ref[...] + _rms(out, g_ref[...], RMS_EPS)


def _merge(ya, yb, yc, pg, x, wa, wb, wc, wo, g, tm=512):
    m, d = x.shape
    tm = _tile(m, tm)
    row = lambda i: (i, 0)
    full = lambda i: (0, 0)
    return pl.pallas_call(
        _merge_kernel,
        out_shape=jax.ShapeDtypeStruct((m, d), F32),
        grid=(m // tm,),
        in_specs=[pl.BlockSpec((tm, ya.shape[1]), row), pl.BlockSpec((tm, yb.shape[1]), row),
                  pl.BlockSpec((tm, yc.shape[1]), row), pl.BlockSpec((tm, 3 * d), row),
                  pl.BlockSpec((tm, d), row),
                  pl.BlockSpec(wa.shape, full), pl.BlockSpec(wb.shape, full), pl.BlockSpec(wc.shape, full),
                  pl.BlockSpec(wo.shape, full), pl.BlockSpec((1, d), full)],
        out_specs=pl.BlockSpec((tm, d), row),
        compiler_params=_params("parallel"),
        name="merge",
    )(ya, yb, yc, pg, x, wa, wb, wc, wo, g)


def _rwkv_stages(pa_ref, sh_ref, s0_ref, mu_ref, w0_ref, w2_ref, a0_ref, a2_ref, g2_ref, kk_ref, ka_ref, rk_ref,
                 lnw_ref, lnb_ref, y_ref, sout_ref, state_ref, carry_ref, *, chunk, t_valid, n_heads, hd):
    c = pl.program_id(1)
    width = n_heads * hd

    @pl.when(c == 0)
    def _():
        state_ref[...] = s0_ref[0]
        carry_ref[...] = sh_ref[0]

    pa = pa_ref[0]
    row = lax.broadcasted_iota(jnp.int32, (chunk, 1), 0)
    prev = jnp.where(row == 0, carry_ref[...], pltpu.roll(pa, 1, axis=0))
    carry_ref[...] = pa[chunk - 1:chunk, :]
    xa = pa + (prev - pa) * mu_ref[...]
    ar = xa[:, 0:width]
    ak = xa[:, width:2 * width]
    av = xa[:, 2 * width:3 * width]
    o = 3 * width
    aw = xa[:, o:o + LORA_W]
    aa = xa[:, o + LORA_W:o + LORA_W + LORA_A]
    ag = xa[:, o + LORA_W + LORA_A:o + LORA_W + LORA_A + LORA_G]

    w_raw = w0_ref[...] + _dot(jnp.tanh(aw), w2_ref[...])
    log_w = -jnp.exp(-jax.nn.softplus(-w_raw) - 0.5)
    a_in = jax.nn.sigmoid(a0_ref[...] + _dot(aa, a2_ref[...]))
    g_out = _dot(jax.nn.sigmoid(ag), g2_ref[...])
    kk_all = ak * kk_ref[...]
    k_all = ak * (1.0 + (a_in - 1.0) * ka_ref[...])
    rk_all = ar * k_all * rk_ref[...]

    masked = t_valid % chunk != 0
    valid = (row + c * chunk) < t_valid
    tri, strict = _tri_masks(chunk)
    pair_masks = _block_pair_masks(chunk)
    tri_f = tri.astype(F32)
    ln_w = lnw_ref[...]
    ln_b = lnb_ref[...]

    heads = range(n_heads)
    sls = [slice(h * hd, (h + 1) * hd) for h in heads]
    lo, hi = slice(0, chunk), slice(chunk, 2 * chunk)

    def prep(sl):
        kk = kk_all[:, sl]
        kk = kk * lax.rsqrt(jnp.sum(kk * kk, axis=-1, keepdims=True) + 1e-12)
        k = k_all[:, sl]
        b_vec = kk * a_in[:, sl]
        lw = log_w[:, sl]
        if masked:
            k = jnp.where(valid, k, 0.0)
            b_vec = jnp.where(valid, b_vec, 0.0)
            lw = jnp.where(valid, lw, 0.0)
        return -kk, b_vec, k, lw

    a_vecs, b_vecs, ks, lws = zip(*[prep(sl) for sl in sls])
    rs = [ar[:, sl] for sl in sls]
    vs = [av[:, sl] for sl in sls]
    yield
    g_incs = [_dot_f32(tri_f, lw) for lw in lws]
    yield
    g_lasts = [g[chunk - 1:chunk, :] for g in g_incs]
    x_mats, y_mats = [], []
    for h in heads:
        e_neg = jnp.exp(-g_incs[h])
        x_mats.append(jnp.concatenate([a_vecs[h] * jnp.exp(g_incs[h] - lws[h]), rs[h] * jnp.exp(g_incs[h])], axis=0))
        y_mats.append(jnp.concatenate([b_vecs[h] * e_neg, ks[h] * e_neg], axis=0))
    mms = [_dot_nt(x_mats[h], y_mats[h]) for h in heads]
    s0s = [state_ref[h] for h in heads]
    xss = [_dot_nt(x_mats[h], s0s[h]) for h in heads]
    yield
    rhss = [xss[h][lo] + _dot(jnp.where(strict, mms[h][lo, hi], 0.0), vs[h]) for h in heads]
    t_qs = yield from _unit_lower_inverse_minus_eye([jnp.where(strict, m[lo, lo], 0.0) for m in mms], pair_masks)
    us = [rhss[h] + _dot(t_qs[h], rhss[h]) for h in heads]
    yield
    ys = [xss[h][hi] + _dot(jnp.where(tri, mms[h][hi, lo], 0.0), us[h])
          + _dot(jnp.where(tri, mms[h][hi, hi], 0.0), vs[h]) for h in heads]
    yield
    for h in heads:
        e_tail = jnp.exp(g_lasts[h] - g_incs[h])
        state_ref[h] = (s0s[h] * jnp.exp(g_lasts[h])
                        + _dot_tn(jnp.concatenate([us[h], vs[h]], axis=0),
                                  jnp.concatenate([b_vecs[h] * e_tail, ks[h] * e_tail], axis=0)))

    outs = []
    for h in heads:
        y, sl = ys[h], sls[h]
        mean = jnp.mean(y, axis=-1, keepdims=True)
        var = jnp.mean(jnp.square(y - mean), axis=-1, keepdims=True)
        yn = (y - mean) * lax.rsqrt(var + GN_EPS_A) * ln_w[:, sl] + ln_b[:, sl]
        bonus = jnp.sum(rk_all[:, sl], axis=-1, keepdims=True) * vs[h]
        outs.append((yn + bonus) * g_out[:, sl])

    y_ref[0] = jnp.concatenate(outs, axis=-1).astype(y_ref.dtype)

    @pl.when(c == pl.num_programs(1) - 1)
    def _():
        sout_ref[0] = state_ref[...]


def _rwkv(pa3, a_shift, states_in, layer_in, state_stack, layer, lp, chunk, nseq):
    nb, t, cols = pa3.shape
    n_heads, hd = states_in.shape[2], states_in.shape[3]
    width = n_heads * hd
    tp = -(-t // chunk) * chunk
    if tp != t:
        pa3 = jnp.pad(pa3, ((0, 0), (0, tp - t), (0, 0)))
    full2 = lambda b, c: (0, 0)
    vec = lambda n: pl.BlockSpec((1, n), full2)
    return dict(
        stages=functools.partial(_rwkv_stages, chunk=chunk, t_valid=t, n_heads=n_heads, hd=hd),
        args=(pa3, a_shift.reshape(nb, 1, cols), states_in, lp["a_mu"], lp["a_w0"], lp["a_w2"], lp["a_a0"],
              lp["a_a2"], lp["a_g2"], lp["a_kk"], lp["a_ka"], lp["a_rk"], lp["a_ln_w"], lp["a_ln_b"]),
        in_kinds=(1, 1, 2) + (0,) * 11,
        out_kinds=(1, 2),
        state_stack=state_stack,
        in_specs=[pl.BlockSpec((nseq, chunk, cols), lambda b, c: (b, c, 0)),
                  pl.BlockSpec((nseq, 1, cols), lambda b, c: (b, 0, 0)),
                  pl.BlockSpec((1, nseq, n_heads, hd, hd), lambda b, c: (layer_in, b, 0, 0, 0)),
                  vec(cols), vec(width), pl.BlockSpec((LORA_W, width), full2),
                  vec(width), pl.BlockSpec((LORA_A, width), full2), pl.BlockSpec((LORA_G, width), full2),
                  vec(width), vec(width), vec(width), vec(width), vec(width)],
        out_shape=[jax.ShapeDtypeStruct((nb, tp, width), BF16), jax.ShapeDtypeStruct(state_stack.shape, F32)],
        out_specs=[pl.BlockSpec((nseq, chunk, width), lambda b, c: (b, c, 0)),
                   pl.BlockSpec((1, nseq, n_heads, hd, hd), lambda b, c: (layer, b, 0, 0, 0))],
        scratch_shapes=[pltpu.VMEM((nseq, n_heads, hd, hd), F32), pltpu.VMEM((nseq, 1, cols), F32)],
        grid=(nb // nseq, tp // chunk))


def _gdn_stages(pc_ref, cv_ref, s0_ref, cw_ref, alog_ref, dtb_ref, nw_ref,
                y_ref, sout_ref, state_ref, carry_ref, *, chunk, t_valid, n_heads, hd):
    c = pl.program_id(1)
    width = n_heads * hd
    taps = CONV_W - 1

    @pl.when(c == 0)
    def _():
        state_ref[...] = s0_ref[0]
        carry_ref[...] = cv_ref[0]

    pc = pc_ref[0]
    x = pc[:, 0:3 * width]
    cz = pc[:, 3 * width:4 * width]
    row = lax.broadcasted_iota(jnp.int32, (chunk, 1), 0)
    conv = x * cw_ref[taps:taps + 1, :]
    for j in range(1, CONV_W):
        shifted = pltpu.roll(x, j, axis=0)
        for i in range(j):
            shifted = jnp.where(row == i, carry_ref[taps - j + i:taps - j + i + 1, :], shifted)
        conv = conv + shifted * cw_ref[taps - j:taps - j + 1, :]
    carry_ref[...] = x[chunk - taps:chunk, :]
    act = jax.nn.silu(conv)
    q_all = act[:, 0:width]
    k_all = act[:, width:2 * width]
    v_all = act[:, 2 * width:3 * width]

    valid = (row + c * chunk) < t_valid
    ps = pc[:, 4 * width:4 * width + LANES]
    beta_all = jnp.where(valid, jax.nn.sigmoid(ps), 0.0)
    lane = lax.broadcasted_iota(jnp.int32, (1, ps.shape[1]), 1)
    dt_full = dtb_ref[...]
    neg_rate = -jnp.exp(alog_ref[...])
    g_all = jnp.where(valid & (lane >= n_heads) & (lane < 2 * n_heads),
                      neg_rate * jax.nn.softplus(ps + dt_full), 0.0)
    tri, strict = _tri_masks(chunk)
    pair_masks = _block_pair_masks(chunk)
    gc_all = _dot_f32(tri.astype(F32), g_all)
    yield
    gc_rows = gc_all.T

    heads = range(n_heads)
    sls = [slice(h * hd, (h + 1) * hd) for h in heads]
    lo, hi = slice(0, chunk), slice(chunk, 2 * chunk)
    qs, ks, vs, kbs, gcols, decays, e_gs, betas = [], [], [], [], [], [], [], []
    for h in heads:
        q = q_all[:, sls[h]]
        qs.append(q * lax.rsqrt(jnp.sum(q * q, axis=-1, keepdims=True) + 1e-6) * (hd ** -0.5))
        k = k_all[:, sls[h]]
        k = k * lax.rsqrt(jnp.sum(k * k, axis=-1, keepdims=True) + 1e-6)
        ks.append(k)
        vs.append(v_all[:, sls[h]])
        beta = beta_all[:, h:h + 1]
        betas.append(beta)
        kbs.append(k * beta)
        gcol = gc_all[:, n_heads + h:n_heads + h + 1]
        grow = gc_rows[n_heads + h:n_heads + h + 1, :]
        gcols.append(gcol)
        decays.append(jnp.where(tri, jnp.exp(jnp.where(tri, gcol - grow, 0.0)), 0.0))
        e_gs.append(jnp.exp(gcol))
    mms = [_dot_nt(jnp.concatenate([kbs[h], qs[h]], axis=0), ks[h]) for h in heads]
    yield
    t_qs = yield from _unit_lower_inverse_minus_eye(
        [jnp.where(strict, -(mms[h][lo] * decays[h]), 0.0) for h in heads], pair_masks)
    rhss = [jnp.concatenate([vs[h] * betas[h], kbs[h] * e_gs[h]], axis=1) for h in heads]
    uws = [rhss[h] + _dot(t_qs[h], rhss[h]) for h in heads]
    yield
    s0s = [state_ref[h] for h in heads]
    wss = [_dot(jnp.concatenate([uws[h][:, hd:2 * hd], qs[h] * e_gs[h]], axis=0), s0s[h]) for h in heads]
    yield
    v_news = [uws[h][:, 0:hd] - wss[h][lo] for h in heads]
    os_ = [wss[h][hi] + _dot(mms[h][hi] * decays[h], v_news[h]) for h in heads]
    yield
    for h in heads:
        g_last = gcols[h][chunk - 1:chunk, :]
        state_ref[h] = s0s[h] * jnp.exp(g_last) + _dot_tn(ks[h] * jnp.exp(g_last - gcols[h]), v_news[h])

    norm_w = nw_ref[...]
    outs = []
    for h in heads:
        o = os_[h]
        on = o * lax.rsqrt(jnp.mean(o * o, axis=-1, keepdims=True) + RMS_EPS) * norm_w
        outs.append(on * jax.nn.silu(cz[:, sls[h]]))

    y_ref[0] = jnp.concatenate(outs, axis=-1).astype(y_ref.dtype)

    @pl.when(c == pl.num_programs(1) - 1)
    def _():
        sout_ref[0] = state_ref[...]


def _gdn(pc3, c_conv, states_in, layer_in, state_stack, layer, lp, chunk, nseq):
    nb, t, cols = pc3.shape
    n_heads, hd = states_in.shape[2], states_in.shape[3]
    width = n_heads * hd
    taps = CONV_W - 1
    tp = -(-t // chunk) * chunk
    if tp != t:
        pc3 = jnp.pad(pc3, ((0, 0), (0, tp - t), (0, 0)))
    full2 = lambda b, c: (0, 0)
    return dict(
        stages=functools.partial(_gdn_stages, chunk=chunk, t_valid=t, n_heads=n_heads, hd=hd),
        args=(pc3, c_conv, states_in, lp["c_conv_w"], lp["c_a_log_pad"], lp["c_dt_bias_pad"], lp["c_norm_w"]),
        in_kinds=(1, 1, 2) + (0,) * 4,
        out_kinds=(1, 2),
        state_stack=state_stack,
        in_specs=[pl.BlockSpec((nseq, chunk, cols), lambda b, c: (b, c, 0)),
                  pl.BlockSpec((nseq, taps, 3 * width), lambda b, c: (b, 0, 0)),
                  pl.BlockSpec((1, nseq, n_heads, hd, hd), lambda b, c: (layer_in, b, 0, 0, 0)),
                  pl.BlockSpec((CONV_W, 3 * width), full2),
                  pl.BlockSpec((1, LANES), full2), pl.BlockSpec((1, LANES), full2),
                  pl.BlockSpec((1, hd), full2)],
        out_shape=[jax.ShapeDtypeStruct((nb, tp, width), BF16), jax.ShapeDtypeStruct(state_stack.shape, F32)],
        out_specs=[pl.BlockSpec((nseq, chunk, width), lambda b, c: (b, c, 0)),
                   pl.BlockSpec((1, nseq, n_heads, hd, hd), lambda b, c: (layer, b, 0, 0, 0))],
        scratch_shapes=[pltpu.VMEM((nseq, n_heads, hd, hd), F32), pltpu.VMEM((nseq, taps, 3 * width), F32)],
        grid=(nb // nseq, tp // chunk))


def _recurrent_kernel(*refs, parts, nseq):
    groups = []
    pos = 0
    for key in ("n_in", "n_out", "n_scratch"):
        group = []
        for part in parts:
            group.append(refs[pos:pos + part[key]])
            pos += part[key]
        groups.append(group)
        if key == "n_in":
            pos += len(parts)
    ins, outs, scratch = groups
    gens = []
    for s in range(nseq):
        one = pl.ds(s, 1)
        view = lambda r, kind: r if kind == 0 else (r.at[one] if kind == 1 else r.at[0].at[one])
        for i, part in enumerate(parts):
            gens.append(part["stages"](
                *[view(r, kind) for r, kind in zip(ins[i], part["in_kinds"])],
                *[view(r, kind) for r, kind in zip(outs[i], part["out_kinds"])],
                *[r.at[s] for r in scratch[i]]))
    _interleave(*gens)


def _recurrent_mixers(specs):
    grid = specs[0]["grid"]
    assert all(s["grid"] == grid for s in specs)
    parts = tuple(dict(stages=s["stages"], n_in=len(s["in_specs"]), n_out=len(s["out_specs"]),
                       n_scratch=len(s["scratch_shapes"]), in_kinds=s["in_kinds"], out_kinds=s["out_kinds"])
                  for s in specs)
    flat = lambda key: [item for s in specs for item in s[key]]
    nseq = specs[0]["in_specs"][0].block_shape[0]
    n_in = len(flat("in_specs"))
    state_out, pos = [], 0
    for s in specs:
        pos += len(s["out_specs"])
        state_out.append(pos - 1)
    outs = pl.pallas_call(
        functools.partial(_recurrent_kernel, parts=parts, nseq=nseq),
        out_shape=flat("out_shape"),
        grid=grid,
        in_specs=flat("in_specs") + [pl.BlockSpec(memory_space=pl.ANY)] * len(specs),
        out_specs=flat("out_specs"),
        scratch_shapes=flat("scratch_shapes"),
        input_output_aliases={n_in + i: o for i, o in enumerate(state_out)},
        compiler_params=_params("parallel", "arbitrary"),
        name="recurrent_mixers",
    )(*flat("args"), *[s["state_stack"] for s in specs])
    results, pos = [], 0
    for s in specs:
        results.append(outs[pos:pos + len(s["out_specs"])])
        pos += len(s["out_specs"])
    return results


def _t5_bucket(dist):
    n = jnp.maximum(dist, 0)
    max_exact = N_BUCKETS // 2
    nf = jnp.maximum(n, 1).astype(F32)
    large = max_exact + (jnp.log(nf / max_exact) / math.log(MAX_DISTANCE / max_exact)
                         * (N_BUCKETS - max_exact)).astype(jnp.int32)
    return jnp.where(n < max_exact, n, jnp.minimum(large, N_BUCKETS - 1))


def _diff_attn_kernel(rb_ref, sc_ref, q_ref, k_ref, v_ref, sub_ref, o_ref, bias_ref, kb_ref, vb_ref, *, blk, hd):
    h = pl.program_id(0)
    qi = pl.program_id(2)
    far_bias = rb_ref[N_BUCKETS - 1, h]

    @pl.when(qi == 0)
    def _():
        kb_ref[...] = k_ref[0].astype(BF16)
        vb_ref[...] = v_ref[0].astype(BF16)

    @pl.when((qi == 0) & (pl.program_id(1) == 0))
    def _():
        r = lax.broadcasted_iota(jnp.int32, (blk, blk), 0)
        c = lax.broadcasted_iota(jnp.int32, (blk, blk), 1)
        for delta in range(2):
            dist = delta * blk + r - c
            bucket = _t5_bucket(dist)
            bias = jnp.zeros((blk, blk), F32)
            for n in range(N_BUCKETS):
                bias = jnp.where(bucket == n, rb_ref[n, h], bias)
            bias_ref[delta] = jnp.where(dist >= 0, bias, NEG_INF)

    q = q_ref[0] * (hd ** -0.5)
    first_half = lax.broadcasted_iota(jnp.int32, (1, 2 * hd), 1) < hd
    qs = (jnp.where(first_half, q, 0.0).astype(BF16), jnp.where(first_half, 0.0, q).astype(BF16))

    def step(kj, bias, carry):
        start = pl.multiple_of(kj * blk, blk)
        kt = kb_ref[pl.ds(start, blk), :]
        vt = vb_ref[pl.ds(start, blk), :]
        maps = range(2)
        ss = [_dot_nt(qs[c], kt) + bias for c in maps]
        m_news = [jnp.maximum(carry[c][0], jnp.max(ss[c], axis=-1, keepdims=True)) for c in maps]
        ps = [jnp.exp(ss[c] - m_news[c]) for c in maps]
        pvs = [jnp.dot(ps[c].astype(BF16), vt, preferred_element_type=F32) for c in maps]
        new = []
        for c in maps:
            m, l, acc = carry[c]
            alpha = jnp.exp(m - m_news[c])
            new.append((m_news[c], alpha * l + jnp.sum(ps[c], axis=-1, keepdims=True), alpha * acc + pvs[c]))
        return tuple(new)

    init = tuple((jnp.full((blk, 1), NEG_INF, F32), jnp.zeros((blk, 1), F32), jnp.zeros((blk, 2 * hd), F32))
                 for _ in range(2))
    carry = lax.fori_loop(0, jnp.maximum(qi - 1, 0), lambda kj, cr: step(kj, far_bias, cr), init)
    carry = lax.cond(qi >= 1, lambda cr: step(qi - 1, bias_ref[1], cr), lambda cr: cr, carry)
    carry = step(qi, bias_ref[0], carry)

    (_, l1, acc1), (_, l2, acc2) = carry
    out = acc1 / l1 - sc_ref[0] * (acc2 / l2)
    out = _rms(out, sub_ref[...], SUBLN_EPS) * sc_ref[1]
    o_ref[0] = out.astype(o_ref.dtype)


def _diff_attn_prompt(pb3, rel_bias, scal, subln):
    nb, t, cols = pb3.shape
    n_heads = rel_bias.shape[1]
    hw = cols // (3 * n_heads)
    blk = next(b for b in (ATT_BLOCK, ATT_BLOCK // 2, MAX_DISTANCE) if t % b == 0)
    assert t % blk == 0 and MAX_DISTANCE <= blk
    smem = pl.BlockSpec(memory_space=pltpu.SMEM)
    return pl.pallas_call(
        functools.partial(_diff_attn_kernel, blk=blk, hd=hw // 2),
        out_shape=jax.ShapeDtypeStruct((nb, t, n_heads * hw), BF16),
        grid=(n_heads, nb, t // blk),
        in_specs=[smem, smem,
                  pl.BlockSpec((1, blk, hw), lambda h, b, i: (b, i, h)),
                  pl.BlockSpec((1, t, hw), lambda h, b, i: (b, 0, n_heads + h)),
                  pl.BlockSpec((1, t, hw), lambda h, b, i: (b, 0, 2 * n_heads + h)),
                  pl.BlockSpec((1, hw), lambda h, b, i: (0, 0))],
        out_specs=pl.BlockSpec((1, blk, hw), lambda h, b, i: (b, i, h)),
        scratch_shapes=[pltpu.VMEM((2, blk, blk), F32), pltpu.VMEM((t, hw), BF16), pltpu.VMEM((t, hw), BF16)],
        compiler_params=_params("arbitrary", "arbitrary", "arbitrary"),
        name="diff_attn_prompt",
    )(rel_bias, scal, pb3, pb3, pb3, subln)


def _paged_attn_kernel(pt_ref, sc_ref, q_ref, kn_ref, vn_ref, rb_ref, sub_ref, *rest, n_heads, hd, page, n_pages):
    k_refs = rest[0:n_pages]
    v_refs = rest[n_pages:2 * n_pages]
    o_ref, bias_ref = rest[2 * n_pages], rest[2 * n_pages + 1]
    n_maps = 2 * n_heads
    rows = page * n_heads
    map_idx = lax.broadcasted_iota(jnp.int32, (n_maps, 1), 0)

    @pl.when(pl.program_id(0) == 0)
    def _():
        lane = lax.broadcasted_iota(jnp.int32, (1, rows), 1)
        own_head = (lane % n_heads) == (map_idx // 2)
        far = jnp.broadcast_to(rb_ref[:, N_BUCKETS - 1:N_BUCKETS], (n_maps, rows))
        bucket = _t5_bucket(page - lane // n_heads)
        near = jnp.zeros((n_maps, rows), F32)
        for n in range(N_BUCKETS):
            near = jnp.where(bucket == n, rb_ref[:, n:n + 1], near)
        bias_ref[0] = jnp.where(own_head, far, NEG_INF)
        bias_ref[1] = jnp.where(own_head, near, NEG_INF)

    own_half = (lax.broadcasted_iota(jnp.int32, (1, 2 * hd), 1) // hd) == (map_idx % 2)
    qm = jnp.where(own_half, q_ref[0] * (hd ** -0.5), 0.0)
    qb = qm.astype(BF16)

    scores = [_dot_nt(qb, k_refs[i][0, 0].astype(BF16)) + bias_ref[1 if i == n_pages - 1 else 0]
              for i in range(n_pages)]
    s_self = jnp.sum(qm * kn_ref[0], axis=1, keepdims=True) + rb_ref[:, 0:1]
    m = s_self
    for s in scores:
        m = jnp.maximum(m, jnp.max(s, axis=1, keepdims=True))
    p_self = jnp.exp(s_self - m)
    l = p_self
    acc = p_self * vn_ref[0]
    for i in range(n_pages):
        p = jnp.exp(scores[i] - m)
        l = l + jnp.sum(p, axis=1, keepdims=True)
        acc = acc + jnp.dot(p.astype(BF16), v_refs[i][0, 0].astype(BF16), preferred_element_type=F32)
    o = acc / l
    d = o - sc_ref[0] * pltpu.roll(o, n_maps - 1, axis=0)
    o_ref[0] = (_rms(d, sub_ref[...], SUBLN_EPS) * sc_ref[1]).astype(o_ref.dtype)


def _diff_attn_sample(pb, cache_k, cache_v, page_table, layer, rel_bias, scal, subln):
    nbs, cols = pb.shape
    n_heads = rel_bias.shape[1]
    width = cols // 3
    hw = width // n_heads
    n_pool, depth, page = cache_k.shape[0], cache_k.shape[1], cache_k.shape[2]
    n_pages = page_table.shape[1]
    assert MAX_DISTANCE <= page
    n_maps = 2 * n_heads
    rows = page * n_heads
    ck = cache_k.reshape(n_pool, depth, rows, hw)
    cv = cache_v.reshape(n_pool, depth, rows, hw)
    qkv = jnp.repeat(pb.reshape(nbs, 3, n_heads, hw), 2, axis=2)
    rb_maps = jnp.repeat(rel_bias, 2, axis=1).T
    per_seq = pl.BlockSpec((1, n_maps, hw), lambda b, pt: (b, 0, 0))
    page_spec = lambda i: pl.BlockSpec((1, 1, rows, hw), lambda b, pt: (pt[b * n_pages + i], layer, 0, 0))
    grid_spec = pltpu.PrefetchScalarGridSpec(
        num_scalar_prefetch=1,
        grid=(nbs,),
        in_specs=[pl.BlockSpec(memory_space=pltpu.SMEM), per_seq, per_seq, per_seq,
                  pl.BlockSpec(rb_maps.shape, lambda b, pt: (0, 0)),
                  pl.BlockSpec((1, hw), lambda b, pt: (0, 0))]
                 + [page_spec(i) for i in range(n_pages)] + [page_spec(i) for i in range(n_pages)],
        out_specs=per_seq,
        scratch_shapes=[pltpu.VMEM((2, n_maps, rows), F32)])
    out = pl.pallas_call(
        functools.partial(_paged_attn_kernel, n_heads=n_heads, hd=hw // 2, page=page, n_pages=n_pages),
        out_shape=jax.ShapeDtypeStruct((nbs, n_maps, hw), BF16),
        grid_spec=grid_spec,
        compiler_params=_params("arbitrary"),
        name="diff_attn_paged",
    )(page_table.reshape(-1), scal, qkv[:, 0], qkv[:, 1], qkv[:, 2], rb_maps, subln,
      *([ck] * n_pages), *([cv] * n_pages))
    return out[:, 0::2].reshape(nbs, width)


def _mem_attn_kernel(q_ref, k_ref, v_ref, o_ref, *, n_heads, hd):
    q = q_ref[0] * (hd ** -0.5)
    heads = range(n_heads)
    qs = [q[:, h * hd:(h + 1) * hd] for h in heads]
    if len(k_ref.shape) == 3:
        k = k_ref[0]
        v = v_ref[0]
        scores = [_dot_nt(qs[h], k[:, h * hd:(h + 1) * hd]) for h in heads]
    else:
        scores = [_dot(qs[h], k_ref[0, 0, h]) for h in heads]
    probs = []
    for s in scores:
        p = jnp.exp(s - jnp.max(s, axis=-1, keepdims=True))
        probs.append(p / jnp.sum(p, axis=-1, keepdims=True))
    if len(k_ref.shape) == 3:
        outs = [_dot(probs[h], v[:, h * hd:(h + 1) * hd]) for h in heads]
    else:
        outs = [_dot_nt(probs[h], v_ref[0, 0, h]) for h in heads]
    o_ref[0] = jnp.concatenate(outs, axis=-1).astype(o_ref.dtype)


def _mem_attn(q3, mk, mv, layer, n_heads, tq=512):
    nb, t, w = q3.shape
    tq = _tile(t, tq)
    if mk.ndim == 3:
        kv_spec = pl.BlockSpec((1,) + mk.shape[1:], lambda b, i: (b, 0, 0))
    else:
        mk = jnp.transpose(mk, (0, 1, 3, 4, 2))
        mv = jnp.transpose(mv, (0, 1, 3, 4, 2))
        kv_spec = pl.BlockSpec((1, 1) + mk.shape[2:], lambda b, i: (layer, b, 0, 0, 0))
    return pl.pallas_call(
        functools.partial(_mem_attn_kernel, n_heads=n_heads, hd=w // n_heads),
        out_shape=jax.ShapeDtypeStruct((nb, t, w), BF16),
        grid=(nb, t // tq),
        in_specs=[pl.BlockSpec((1, tq, w), lambda b, i: (b, i, 0)), kv_spec, kv_spec],
        out_specs=pl.BlockSpec((1, tq, w), lambda b, i: (b, i, 0)),
        compiler_params=_params("parallel", "arbitrary"),
        name="mem_attn",
    )(q3, mk, mv)


def _step_kernel(pa_ref, sh_ref, pc_ref, cv_ref, sa_ref, sc_ref,
                 mu_ref, w0_ref, w2_ref, a0_ref, a2_ref, g2_ref, kk_ref, ka_ref, rk_ref, lnw_ref, lnb_ref,
                 cw_ref, alog_ref, dtb_ref, nw_ref, sa_alias, sc_alias,
                 ya_ref, yc_ref, sa_out, sc_out,
                 at_ref, ct_ref, pst_ref, yat_ref, yct_ref, col_ref, *, n_heads, hd):
    del sa_alias, sc_alias
    h = pl.program_id(0)
    width = n_heads * hd
    nb = pa_ref.shape[0]
    rows = pl.ds(pl.multiple_of(h * hd, hd), hd)
    colsum = lambda x: jnp.sum(x, axis=0, keepdims=True)

    pa = pa_ref[...]
    xa = pa + (sh_ref[...] - pa) * mu_ref[...]
    o = 3 * width
    lora_w = _dot(jnp.tanh(xa[:, o:o + LORA_W]), w2_ref[...])
    lora_a = _dot(xa[:, o + LORA_W:o + LORA_W + LORA_A], a2_ref[...])
    g_out = _dot(jax.nn.sigmoid(xa[:, o + LORA_W + LORA_A:o + LORA_W + LORA_A + LORA_G]), g2_ref[...])

    @pl.when(h == 0)
    def _():
        ar, ak, av = xa[:, 0:width], xa[:, width:2 * width], xa[:, 2 * width:3 * width]
        decay = jnp.exp(-jnp.exp(-jax.nn.softplus(-(w0_ref[...] + lora_w)) - 0.5))
        a_in = jax.nn.sigmoid(a0_ref[...] + lora_a)
        k_all = ak * (1.0 + (a_in - 1.0) * ka_ref[...])
        for i, arr in enumerate((ar, decay, k_all, av, ak * kk_ref[...], a_in, g_out, ar * k_all * rk_ref[...])):
            at_ref[i] = arr.T
        pc = pc_ref[...]
        taps = CONV_W - 1
        conv = pc[:, 0:3 * width] * cw_ref[taps:taps + 1, :]
        for j in range(taps):
            conv = conv + cv_ref[:, j, :] * cw_ref[j:j + 1, :]
        act = jax.nn.silu(conv)
        for i, arr in enumerate((act[:, 0:width], act[:, width:2 * width], act[:, 2 * width:3 * width],
                                 jax.nn.silu(pc[:, 3 * width:4 * width]))):
            ct_ref[i] = arr.T
        ps = pc[:, 4 * width:4 * width + LANES]
        gate = -jnp.exp(alog_ref[...]) * jax.nn.softplus(ps + dtb_ref[...])
        lane = lax.broadcasted_iota(jnp.int32, (1, LANES), 1)
        pst_ref[...] = jnp.where(lane < n_heads, jax.nn.sigmoid(ps), gate).T

    r, w, k, v = at_ref[0, rows, :], at_ref[1, rows, :], at_ref[2, rows, :], at_ref[3, rows, :]
    kk = at_ref[4, rows, :]
    kk = kk * lax.rsqrt(colsum(kk * kk) + 1e-12)
    a_vec = -kk
    b_vec = kk * at_ref[5, rows, :]
    col_ref[0] = v

    def a_row(i, carry):
        s_row = sa_ref[0, 0, i]
        s_a = colsum(s_row * a_vec)
        s_new = s_row * w + s_a * b_vec + col_ref[0, pl.ds(i, 1), :] * k
        sa_out[0, 0, i] = s_new
        col_ref[1, pl.ds(i, 1), :] = colsum(s_new * r)
        return carry

    lax.fori_loop(0, hd, a_row, 0, unroll=4)
    y = col_ref[1]
    mean = colsum(y) * (1.0 / hd)
    var = colsum(jnp.square(y - mean)) * (1.0 / hd)
    yn = (y - mean) * lax.rsqrt(var + GN_EPS_A) * lnw_ref[rows, :] + lnb_ref[rows, :]
    bonus = colsum(at_ref[7, rows, :]) * v
    yat_ref[rows, :] = (yn + bonus) * at_ref[6, rows, :]

    q, kc, vc = ct_ref[0, rows, :], ct_ref[1, rows, :], ct_ref[2, rows, :]
    q = q * lax.rsqrt(colsum(q * q) + 1e-6) * (hd ** -0.5)
    kc = kc * lax.rsqrt(colsum(kc * kc) + 1e-6)
    beta = pst_ref[pl.ds(h, 1), :]
    g = pst_ref[pl.ds(n_heads + h, 1), :]
    e_g = jnp.exp(g)
    col_ref[2] = kc * (beta * e_g)
    col_ref[3] = q * e_g
    col_ref[4] = kc

    def c_reduce(i, carry):
        t_w, t_q = carry
        s_row = sc_ref[0, 0, i]
        return (t_w + col_ref[2, pl.ds(i, 1), :] * s_row, t_q + col_ref[3, pl.ds(i, 1), :] * s_row)

    zero = jnp.zeros((hd, nb), F32)
    t_w, t_q = lax.fori_loop(0, hd, c_reduce, (zero, zero), unroll=4)
    v_new = vc * beta - t_w
    o = t_q + colsum(q * kc) * v_new

    def c_update(i, carry):
        sc_out[0, 0, i] = sc_ref[0, 0, i] * e_g + col_ref[4, pl.ds(i, 1), :] * v_new
        return carry

    lax.fori_loop(0, hd, c_update, 0, unroll=4)
    on = o * lax.rsqrt(colsum(o * o) * (1.0 / hd) + RMS_EPS) * nw_ref[...]
    yct_ref[rows, :] = on * ct_ref[3, rows, :]

    @pl.when(h == n_heads - 1)
    def _():
        ya_ref[...] = yat_ref[...].T.astype(ya_ref.dtype)
        yc_ref[...] = yct_ref[...].T.astype(yc_ref.dtype)


def _step_mixers(pa, a_shift, pc, c_conv, wkv_t, delta_t, wkv_stack, delta_stack, layer, lp):
    nb, a_cols = pa.shape
    depth, n_heads, hd = wkv_t.shape[0], wkv_t.shape[1], wkv_t.shape[2]
    width = n_heads * hd
    col = lambda v: v.reshape(-1, 1)
    params = (lp["a_mu"], lp["a_w0"], lp["a_w2"], lp["a_a0"], lp["a_a2"], lp["a_g2"], lp["a_kk"], lp["a_ka"],
              lp["a_rk"], col(lp["a_ln_w"]), col(lp["a_ln_b"]), lp["c_conv_w"], lp["c_a_log_pad"],
              lp["c_dt_bias_pad"], col(lp["c_norm_w"]))
    whole = lambda a: pl.BlockSpec(a.shape, lambda h: (0,) * a.ndim)
    state_in = pl.BlockSpec((1, 1, hd, hd, nb), lambda h: (layer, h, 0, 0, 0))
    any_spec = pl.BlockSpec(memory_space=pl.ANY)
    ya, yc, new_wkv, new_delta = pl.pallas_call(
        functools.partial(_step_kernel, n_heads=n_heads, hd=hd),
        out_shape=(jax.ShapeDtypeStruct((nb, width), BF16), jax.ShapeDtypeStruct((nb, width), BF16),
                   jax.ShapeDtypeStruct(wkv_stack.shape, F32), jax.ShapeDtypeStruct(delta_stack.shape, F32)),
        grid=(n_heads,),
        in_specs=[whole(pa), whole(a_shift), whole(pc), whole(c_conv), state_in, state_in]
                 + [whole(a) for a in params] + [any_spec, any_spec],
        out_specs=(pl.BlockSpec((nb, width), lambda h: (0, 0)), pl.BlockSpec((nb, width), lambda h: (0, 0)),
                   state_in, state_in),
        scratch_shapes=[pltpu.VMEM((8, width, nb), F32), pltpu.VMEM((4, width, nb), F32),
                        pltpu.VMEM((LANES, nb), F32), pltpu.VMEM((width, nb), F32), pltpu.VMEM((width, nb), F32),
                        pltpu.VMEM((5, hd, nb), F32)],
        input_output_aliases={6 + len(params): 2, 7 + len(params): 3},
        compiler_params=_params("arbitrary"),
        name="step_mixers",
    )(pa, a_shift, pc, c_conv, wkv_t, delta_t, *params, wkv_stack, delta_stack)
    return ya, yc, new_wkv, new_delta


def _layer(lp, layer, x, mem_k, mem_v, a_shift, a_wkv, c_conv, c_delta, paged, kv_stacks):
    nb, t, d = x.shape
    m = nb * t
    x2 = x.reshape(m, d)
    g = lp["g_pre_mix"]
    pa = _mm(x2, lp["w_in_a"], g, tn=896)
    pb, new_bk, new_bv = _mm_qkv(x2, lp["w_in_b"], g, nb, t, layer, *kv_stacks)
    pc = _mm(x2, lp["w_in_c"], g, tn=lp["w_in_c"].shape[1])
    pg = _mm(x2, lp["w_in_g"], g)

    chunk = CHUNK if t >= CHUNK else SUBLANES
    pa3 = pa.reshape(nb, t, -1)
    pc3 = pc.reshape(nb, t, -1)
    if t == 1:
        assert a_wkv[1] == layer and c_delta[1] == layer
        ya, yc, new_wkv, new_delta = _step_mixers(pa, a_shift, pc, c_conv, a_wkv[0], c_delta[0],
                                                  a_wkv[2], c_delta[2], layer, lp)
    else:
        want = SEQS_PER_STEP_LONG if t >= CHUNK else SEQS_PER_STEP_SHORT
        nseq = max(n for n in range(1, want + 1) if nb % n == 0)
        (ya, new_wkv), (yc, new_delta) = _recurrent_mixers(
            [_rwkv(pa3, a_shift, *a_wkv, layer, lp, chunk, nseq),
             _gdn(pc3, c_conv, *c_delta, layer, lp, chunk, nseq)])
        ya, yc = ya[:, :t], yc[:, :t]
    new_shift = pa3[:, t - 1]

    if paged is None:
        yb = _diff_attn_prompt(pb.reshape(nb, t, -1), lp["rel_bias"], lp["b_scal"], lp["b_subln"])
    else:
        cache_k, cache_v, page_table = paged
        yb = _diff_attn_sample(pb, cache_k, cache_v, page_table, layer, lp["rel_bias"], lp["b_scal"], lp["b_subln"])
    w_c3 = c_conv.shape[2]
    new_conv = jnp.concatenate([c_conv, pc3[:, :, 0:w_c3]], axis=1)[:, t:]

    x2 = _merge(ya.reshape(m, -1), yb.reshape(m, -1), yc.reshape(m, -1), pg, x2,
                lp["w_br_a"], lp["w_br_b"], lp["w_br_c"], lp["w_out"], lp["g_post_mix"])

    q = _mm(x2, lp["w_mem_q"], lp["g_pre_mem"])
    att = _mem_attn(q.reshape(nb, t, -1), mem_k, mem_v, layer, lp["n_heads_mem"])
    x2 = _mm_post(att.reshape(m, -1), lp["w_mem_o"], lp["g_post_mem"], x2)

    hf = _mm(x2, lp["w_ffn1"], lp["g_pre_ffn"], act="relu2", out_dtype=BF16)
    x2 = _mm_post(hf, lp["w_ffn2"], lp["g_post_ffn"], x2)
    return x2.reshape(nb, t, d), (new_shift, new_wkv, new_conv, new_delta, new_bk, new_bv)


def _layer_params(p, l):
    d = p["w_in"].shape[1]
    n_ha, hd_a = p["a_rk"].shape[1], p["a_rk"].shape[2]
    w_a = n_ha * hd_a
    a_cols = 3 * w_a + LORA_W + LORA_A + LORA_G
    w_b = p["w_br_b"].shape[1]
    w_c = p["w_br_c"].shape[1]
    n_hc = p["c_a_log"].shape[1]
    b_cols = 3 * w_b
    row = lambda v: v.reshape(1, -1).astype(F32)
    w_in = p["w_in"][l]
    o_b = a_cols
    o_c = o_b + b_cols
    o_s = o_c + 4 * w_c
    o_g = o_s + 2 * n_hc
    lam_init = 0.8 - 0.6 * math.exp(-0.3 * l)
    lam = (jnp.exp(jnp.sum(p["b_lam_q1"][l] * p["b_lam_k1"][l]))
           - jnp.exp(jnp.sum(p["b_lam_q2"][l] * p["b_lam_k2"][l])) + lam_init)
    pad_heads = lambda v: jnp.zeros((1, LANES), F32).at[0, n_hc:2 * n_hc].set(v)
    lp = {
        "w_in_a": w_in[:, 0:o_b].astype(BF16),
        "w_in_b": w_in[:, o_b:o_c].astype(BF16),
        "w_in_c": jnp.pad(w_in[:, o_c:o_g], ((0, 0), (0, LANES - 2 * n_hc))).astype(BF16),
        "w_in_g": w_in[:, o_g:o_g + 3 * d].astype(BF16),
        "a_mu": row(p["a_mu"][l]), "a_w0": row(p["a_w0"][l]), "a_w2": p["a_w2"][l].astype(BF16),
        "a_a0": row(p["a_a0"][l]), "a_a2": p["a_a2"][l].astype(BF16), "a_g2": p["a_g2"][l].astype(BF16),
        "a_kk": row(p["a_kk"][l]), "a_ka": row(p["a_ka"][l]), "a_rk": row(p["a_rk"][l]),
        "a_ln_w": row(p["a_ln_w"][l]), "a_ln_b": row(p["a_ln_b"][l]),
        "rel_bias": p["rel_bias"].astype(F32),
        "b_scal": jnp.stack([lam, jnp.asarray(1.0 - lam_init, F32)]).astype(F32),
        "b_subln": row(p["b_subln"][l]),
        "c_conv_w": p["c_conv_w"][l].astype(F32),
        "c_a_log_pad": pad_heads(p["c_a_log"][l]), "c_dt_bias_pad": pad_heads(p["c_dt_bias"][l]),
        "c_norm_w": row(p["c_norm_w"][l]),
        "w_br_a": p["w_br_a"][l].astype(BF16), "w_br_b": p["w_br_b"][l].astype(BF16),
        "w_br_c": p["w_br_c"][l].astype(BF16), "w_out": p["w_out"][l].astype(BF16),
        "g_mem": row(p["g_mem"][l]), "w_mem_q": p["w_mem_q"][l].astype(BF16),
        "w_mem_kv": p["w_mem_kv"][l].astype(BF16), "w_mem_o": p["w_mem_o"][l].astype(BF16),
        "n_heads_mem": p["n_heads_mem"],
        "w_ffn1": p["w_ffn1"][l].astype(BF16), "w_ffn2": p["w_ffn2"][l].astype(BF16),
    }
    for name in ("g_pre_mix", "g_post_mix", "g_pre_mem", "g_post_mem", "g_pre_ffn", "g_post_ffn"):
        lp[name] = row(p[name][l])
    return lp


def kernel(x_prompt, x_sample, mem_prompt, cache_b_k, cache_b_v, page_table, state_a_wkv, state_a_shift, state_c_delta, state_c_conv, cache_mem_k, cache_mem_v, g_pre_mix, g_post_mix, g_pre_mem, g_post_mem, g_pre_ffn, g_post_ffn, w_in, a_mu, a_w0, a_w2, a_a0, a_a2, a_g2, a_kk, a_ka, a_rk, a_ln_w, a_ln_b, b_lam_q1, b_lam_k1, b_lam_q2, b_lam_k2, b_subln, rel_bias, c_conv_w, c_a_log, c_dt_bias, c_norm_w, w_br_a, w_br_b, w_br_c, w_out, g_mem, w_mem_q, w_mem_kv, w_mem_o, w_ffn1, w_ffn2):
    p = dict(g_pre_mix=g_pre_mix, g_post_mix=g_post_mix, g_pre_mem=g_pre_mem, g_post_mem=g_post_mem,
             g_pre_ffn=g_pre_ffn, g_post_ffn=g_post_ffn, w_in=w_in,
             a_mu=a_mu, a_w0=a_w0, a_w2=a_w2, a_a0=a_a0, a_a2=a_a2, a_g2=a_g2, a_kk=a_kk, a_ka=a_ka,
             a_rk=a_rk, a_ln_w=a_ln_w, a_ln_b=a_ln_b,
             b_lam_q1=b_lam_q1, b_lam_k1=b_lam_k1, b_lam_q2=b_lam_q2, b_lam_k2=b_lam_k2,
             b_subln=b_subln, rel_bias=rel_bias,
             c_conv_w=c_conv_w, c_a_log=c_a_log, c_dt_bias=c_dt_bias, c_norm_w=c_norm_w,
             w_br_a=w_br_a, w_br_b=w_br_b, w_br_c=w_br_c, w_out=w_out,
             g_mem=g_mem, w_mem_q=w_mem_q, w_mem_kv=w_mem_kv, w_mem_o=w_mem_o,
             w_ffn1=w_ffn1, w_ffn2=w_ffn2)
    depth = w_in.shape[0]
    nbp = x_prompt.shape[0]
    n_ha, hd_a = a_rk.shape[1], a_rk.shape[2]
    a_cols = a_mu.shape[1]
    n_hc = c_a_log.shape[1]
    hd_c = c_norm_w.shape[1]
    n_hm = cache_mem_k.shape[3]
    p["n_heads_mem"] = n_hm
    n_mem = mem_prompt.shape[1]
    taps = c_conv_w.shape[1] - 1

    xp, xs = x_prompt, x_sample
    outs_p, outs_s, mem_kv = [], [], []
    kv_shape = lambda x: (x.shape[0], depth, x.shape[1]) + cache_b_k.shape[3:]
    kv_p = (jnp.zeros(kv_shape(xp), F32), jnp.zeros(kv_shape(xp), F32))
    kv_s = (jnp.zeros(kv_shape(xs), F32), jnp.zeros(kv_shape(xs), F32))
    zero_wkv = jnp.zeros((1, nbp, n_ha, hd_a, hd_a), F32)
    zero_delta = jnp.zeros((1, nbp, n_hc, hd_c, hd_c), F32)
    wkv_p = jnp.zeros((depth,) + zero_wkv.shape[1:], F32)
    delta_p = jnp.zeros((depth,) + zero_delta.shape[1:], F32)
    single = xs.shape[1] == 1
    to_batch_last = lambda a: jnp.transpose(a, (0, 2, 3, 4, 1)) if single else a
    from_batch_last = lambda a: jnp.transpose(a, (0, 4, 1, 2, 3)) if single else a
    wkv_in, delta_in = to_batch_last(state_a_wkv), to_batch_last(state_c_delta)
    wkv_s = jnp.zeros(wkv_in.shape, F32)
    delta_s = jnp.zeros(delta_in.shape, F32)
    for l in range(depth):
        lp = _layer_params(p, l)
        mkv = _mm(mem_prompt.reshape(nbp * n_mem, -1), lp["w_mem_kv"], lp["g_mem"])
        w_mem = mkv.shape[1] // 2
        mk = mkv[:, 0:w_mem].reshape(nbp, n_mem, w_mem)
        mv = mkv[:, w_mem:2 * w_mem].reshape(nbp, n_mem, w_mem)
        mem_kv.append((mk.reshape(nbp, n_mem, n_hm, -1), mv.reshape(nbp, n_mem, n_hm, -1)))
        xp, st = _layer(lp, l, xp, mk, mv,
                        jnp.zeros((nbp, a_cols), F32), (zero_wkv, 0, wkv_p),
                        jnp.zeros((nbp, taps, 3 * n_hc * hd_c), F32), (zero_delta, 0, delta_p),
                        None, kv_p)
        outs_p.append(st)
        kv_p, wkv_p, delta_p = st[4:6], st[1], st[3]
        xs, st = _layer(lp, l, xs, cache_mem_k, cache_mem_v,
                        state_a_shift[l], (wkv_in, l, wkv_s), state_c_conv[l], (delta_in, l, delta_s),
                        (cache_b_k, cache_b_v, page_table), kv_s)
        outs_s.append(st)
        wkv_s, delta_s = st[1], st[3]
        kv_s = st[4:6]

    stack = lambda group, i, axis: jnp.stack([st[i] for st in group], axis=axis)
    return (xp, xs, kv_p[0], kv_p[1], kv_s[0], kv_s[1],
            wkv_p, from_batch_last(wkv_s), stack(outs_p, 0, 0), stack(outs_s, 0, 0),
            delta_p, from_batch_last(delta_s), stack(outs_p, 2, 0), stack(outs_s, 2, 0),
            jnp.stack([m[0] for m in mem_kv], axis=0), jnp.stack([m[1] for m in mem_kv], axis=0))
```

```python
import functools
import math

import jax
import jax.numpy as jnp
from jax import lax
from jax.experimental import pallas as pl
from jax.experimental.pallas import tpu as pltpu

F32 = jnp.float32
BF16 = jnp.bfloat16

RMS_EPS = 1e-6
GN_EPS_A = 64e-5
SUBLN_EPS = 1e-5
N_BUCKETS = 32
MAX_DISTANCE = 128
CONV_W = 4
LORA_W = 64
LORA_A = 64
LORA_G = 128

LANES = 128
SUBLANES = 8
VMEM_LIMIT_BYTES = 56 * 1024 * 1024

CHUNK = 128
SEQS_PER_STEP_LONG = 1
SEQS_PER_STEP_SHORT = 4
ATT_BLOCK = 512
NEG_INF = -1e30


def _params(*sem):
    return pltpu.CompilerParams(dimension_semantics=sem, vmem_limit_bytes=VMEM_LIMIT_BYTES)


def _tile(n, pref):
    if n <= pref:
        return n
    for t in range(pref, 7, -1):
        if n % t == 0 and t % SUBLANES == 0:
            return t
    return n


def _dot(a, b):
    return jnp.dot(a.astype(BF16), b.astype(BF16), preferred_element_type=F32)


def _dot_nt(a, b):
    return lax.dot_general(a.astype(BF16), b.astype(BF16), (((1,), (1,)), ((), ())), preferred_element_type=F32)


def _dot_tn(a, b):
    return lax.dot_general(a.astype(BF16), b.astype(BF16), (((0,), (0,)), ((), ())), preferred_element_type=F32)


def _dot_f32(a, b):
    return jnp.dot(a, b, preferred_element_type=F32, precision=lax.Precision.HIGHEST)


def _rms(x, g, eps):
    return x * lax.rsqrt(jnp.mean(x * x, axis=-1, keepdims=True) + eps) * g


def _tri_masks(n):
    r = lax.broadcasted_iota(jnp.int32, (n, n), 0)
    c = lax.broadcasted_iota(jnp.int32, (n, n), 1)
    return r >= c, r > c


def _block_pair_masks(n):
    r = lax.broadcasted_iota(jnp.int32, (n, n), 0)
    c = lax.broadcasted_iota(jnp.int32, (n, n), 1)
    masks = []
    s = 1
    while s < n:
        masks.append(((r // (2 * s)) == (c // (2 * s))) & ((r // s) % 2 == 1) & ((c // s) % 2 == 0))
        s *= 2
    return masks


def _unit_lower_inverse_minus_eye(nmats, masks):
    qs = [jnp.where(masks[0], n, 0.0) for n in nmats]
    for mask in masks[1:]:
        n_s = [jnp.where(mask, n, 0.0) for n in nmats]
        ys = [n + _dot(q, n) for q, n in zip(qs, n_s)]
        yield
        qs = [q + y + _dot(y, q) for q, y in zip(qs, ys)]
        yield
    return qs


def _interleave(*stage_generators):
    live = list(stage_generators)
    while live:
        for gen in list(live):
            try:
                next(gen)
            except StopIteration:
                live.remove(gen)


def _mm_kernel(x_ref, g_ref, w_ref, o_ref, xn_ref, *, norm, act):
    @pl.when(pl.program_id(1) == 0)
    def _():
        if norm:
            xn_ref[...] = _rms(x_ref[...].astype(F32), g_ref[...], RMS_EPS).astype(BF16)
        else:
            xn_ref[...] = x_ref[...].astype(BF16)

    y = jnp.dot(xn_ref[...], w_ref[...], preferred_element_type=F32)
    if act == "relu2":
        y = jnp.square(jnp.maximum(y, 0.0))
    o_ref[...] = y.astype(o_ref.dtype)


def _mm(x, w, g=None, act=None, out_dtype=F32, tm=1024, tn=1024):
    m, k = x.shape
    n = w.shape[1]
    tm = _tile(m, tm)
    tn = _tile(n, tn)
    norm = g is not None
    if g is None:
        g = jnp.ones((1, k), F32)
    return pl.pallas_call(
        functools.partial(_mm_kernel, norm=norm, act=act),
        out_shape=jax.ShapeDtypeStruct((m, n), out_dtype),
        grid=(m // tm, n // tn),
        in_specs=[pl.BlockSpec((tm, k), lambda i, j: (i, 0)),
                  pl.BlockSpec((1, k), lambda i, j: (0, 0)),
                  pl.BlockSpec((k, tn), lambda i, j: (0, j))],
        out_specs=pl.BlockSpec((tm, tn), lambda i, j: (i, j)),
        scratch_shapes=[pltpu.VMEM((tm, k), BF16)],
        compiler_params=_params("parallel", "arbitrary"),
        name="mm_norm",
    )(x, g, w)


def _mm_qkv_kernel(x_ref, g_ref, w_ref, k_in, v_in, o_ref, k_out, v_out, xn_ref, *, n_heads, rows_are_tokens):
    del k_in, v_in
    j = pl.program_id(1)

    @pl.when(j == 0)
    def _():
        xn_ref[...] = _rms(x_ref[...].astype(F32), g_ref[...], RMS_EPS).astype(BF16)

    y = jnp.dot(xn_ref[...], w_ref[...], preferred_element_type=F32)
    o_ref[...] = y
    hw = y.shape[1] // n_heads

    def split_heads(dst):
        for h in range(n_heads):
            if rows_are_tokens:
                dst[0, 0, :, h, :] = y[:, h * hw:(h + 1) * hw]
            else:
                dst[:, 0, 0, h, :] = y[:, h * hw:(h + 1) * hw]

    @pl.when(j == 1)
    def _():
        split_heads(k_out)

    @pl.when(j == 2)
    def _():
        split_heads(v_out)


def _mm_qkv(x, w, g, nb, t, layer, k_stack, v_stack, tm=1024):
    m, k = x.shape
    tn = w.shape[1] // 3
    _, depth, _, n_heads, hw = k_stack.shape
    assert n_heads * hw == tn
    tm = _tile(t, tm) if t > 1 else _tile(m, tm)
    if t > 1:
        per_seq = t // tm
        kv_spec = pl.BlockSpec((1, 1, tm, n_heads, hw), lambda i, j: (i // per_seq, layer, i % per_seq, 0, 0))
    else:
        kv_spec = pl.BlockSpec((tm, 1, 1, n_heads, hw), lambda i, j: (i, layer, 0, 0, 0))
    any_spec = pl.BlockSpec(memory_space=pl.ANY)
    return pl.pallas_call(
        functools.partial(_mm_qkv_kernel, n_heads=n_heads, rows_are_tokens=t > 1),
        out_shape=(jax.ShapeDtypeStruct((m, 3 * tn), F32),
                   jax.ShapeDtypeStruct(k_stack.shape, F32), jax.ShapeDtypeStruct(v_stack.shape, F32)),
        grid=(m // tm, 3),
        in_specs=[pl.BlockSpec((tm, k), lambda i, j: (i, 0)),
                  pl.BlockSpec((1, k), lambda i, j: (0, 0)),
                  pl.BlockSpec((k, tn), lambda i, j: (0, j)),
                  any_spec, any_spec],
        out_specs=(pl.BlockSpec((tm, tn), lambda i, j: (i, j)), kv_spec, kv_spec),
        scratch_shapes=[pltpu.VMEM((tm, k), BF16)],
        input_output_aliases={3: 1, 4: 2},
        compiler_params=_params("parallel", "arbitrary"),
        name="mm_qkv",
    )(x, g, w, k_stack, v_stack)


def _mm_post_kernel(y_ref, w_ref, g_ref, res_ref, o_ref, acc_ref):
    kk = pl.program_id(1)

    @pl.when(kk == 0)
    def _():
        acc_ref[...] = jnp.zeros_like(acc_ref)

    acc_ref[...] += jnp.dot(y_ref[...].astype(BF16), w_ref[...], preferred_element_type=F32)

    @pl.when(kk == pl.num_programs(1) - 1)
    def _():
        o_ref[...] = res_ref[...] + _rms(acc_ref[...], g_ref[...], RMS_EPS)


def _mm_post(y, w, g, res, tm=512, tk=4096):
    m, k = y.shape
    n = w.shape[1]
    tm = _tile(m, tm)
    tk = _tile(k, tk)
    return pl.pallas_call(
        _mm_post_kernel,
        out_shape=jax.ShapeDtypeStruct((m, n), F32),
        grid=(m // tm, k // tk),
        in_specs=[pl.BlockSpec((tm, tk), lambda i, j: (i, j)),
                  pl.BlockSpec((tk, n), lambda i, j: (j, 0)),
                  pl.BlockSpec((1, n), lambda i, j: (0, 0)),
                  pl.BlockSpec((tm, n), lambda i, j: (i, 0))],
        out_specs=pl.BlockSpec((tm, n), lambda i, j: (i, 0)),
        scratch_shapes=[pltpu.VMEM((tm, n), F32)],
        compiler_params=_params("parallel", "arbitrary"),
        name="mm_post",
    )(y, w, g, res)


def _merge_kernel(ya_ref, yb_ref, yc_ref, pg_ref, x_ref, wa_ref, wb_ref, wc_ref, wo_ref, g_ref, o_ref):
    d = x_ref.shape[1]
    gates = jax.nn.sigmoid(pg_ref[...])
    merged = (gates[:, 0:d] * jnp.dot(ya_ref[...], wa_ref[...], preferred_element_type=F32)
              + gates[:, d:2 * d] * jnp.dot(yb_ref[...], wb_ref[...], preferred_element_type=F32)
              + gates[:, 2 * d:3 * d] * jnp.dot(yc_ref[...], wc_ref[...], preferred_element_type=F32))
    out = jnp.dot(merged.astype(BF16), wo_ref[...], preferred_element_type=F32)
    o_ref[...] = x_ref[...] + _rms(out, g_ref[...], RMS_EPS)


def _merge(ya, yb, yc, pg, x, wa, wb, wc, wo, g, tm=512):
    m, d = x.shape
    tm = _tile(m, tm)
    row = lambda i: (i, 0)
    full = lambda i: (0, 0)
    return pl.pallas_call(
        _merge_kernel,
        out_shape=jax.ShapeDtypeStruct((m, d), F32),
        grid=(m // tm,),
        in_specs=[pl.BlockSpec((tm, ya.shape[1]), row), pl.BlockSpec((tm, yb.shape[1]), row),
                  pl.BlockSpec((tm, yc.shape[1]), row), pl.BlockSpec((tm, 3 * d), row),
                  pl.BlockSpec((tm, d), row),
                  pl.BlockSpec(wa.shape, full), pl.BlockSpec(wb.shape, full), pl.BlockSpec(wc.shape, full),
                  pl.BlockSpec(wo.shape, full), pl.BlockSpec((1, d), full)],
        out_specs=pl.BlockSpec((tm, d), row),
        compiler_params=_params("parallel"),
        name="merge",
    )(ya, yb, yc, pg, x, wa, wb, wc, wo, g)


def _rwkv_stages(pa_ref, sh_ref, s0_ref, mu_ref, w0_ref, w2_ref, a0_ref, a2_ref, g2_ref, kk_ref, ka_ref, rk_ref,
                 lnw_ref, lnb_ref, y_ref, sout_ref, state_ref, carry_ref, *, chunk, t_valid, n_heads, hd):
    c = pl.program_id(1)
    width = n_heads * hd

    @pl.when(c == 0)
    def _():
        state_ref[...] = s0_ref[0]
        carry_ref[...] = sh_ref[0]

    pa = pa_ref[0]
    row = lax.broadcasted_iota(jnp.int32, (chunk, 1), 0)
    prev = jnp.where(row == 0, carry_ref[...], pltpu.roll(pa, 1, axis=0))
    carry_ref[...] = pa[chunk - 1:chunk, :]
    xa = pa + (prev - pa) * mu_ref[...]
    ar = xa[:, 0:width]
    ak = xa[:, width:2 * width]
    av = xa[:, 2 * width:3 * width]
    o = 3 * width
    aw = xa[:, o:o + LORA_W]
    aa = xa[:, o + LORA_W:o + LORA_W + LORA_A]
    ag = xa[:, o + LORA_W + LORA_A:o + LORA_W + LORA_A + LORA_G]

    w_raw = w0_ref[...] + _dot(jnp.tanh(aw), w2_ref[...])
    log_w = -jnp.exp(-jax.nn.softplus(-w_raw) - 0.5)
    a_in = jax.nn.sigmoid(a0_ref[...] + _dot(aa, a2_ref[...]))
    g_out = _dot(jax.nn.sigmoid(ag), g2_ref[...])
    kk_all = ak * kk_ref[...]
    k_all = ak * (1.0 + (a_in - 1.0) * ka_ref[...])
    rk_all = ar * k_all * rk_ref[...]

    masked = t_valid % chunk != 0
    valid = (row + c * chunk) < t_valid
    tri, strict = _tri_masks(chunk)
    pair_masks = _block_pair_masks(chunk)
    tri_f = tri.astype(F32)
    ln_w = lnw_ref[...]
    ln_b = lnb_ref[...]

    heads = range(n_heads)
    sls = [slice(h * hd, (h + 1) * hd) for h in heads]
    lo, hi = slice(0, chunk), slice(chunk, 2 * chunk)

    def prep(sl):
        kk = kk_all[:, sl]
        kk = kk * lax.rsqrt(jnp.sum(kk * kk, axis=-1, keepdims=True) + 1e-12)
        k = k_all[:, sl]
        b_vec = kk * a_in[:, sl]
        lw = log_w[:, sl]
        if masked:
            k = jnp.where(valid, k, 0.0)
            b_vec = jnp.where(valid, b_vec, 0.0)
            lw = jnp.where(valid, lw, 0.0)
        return -kk, b_vec, k, lw

    a_vecs, b_vecs, ks, lws = zip(*[prep(sl) for sl in sls])
    rs = [ar[:, sl] for sl in sls]
    vs = [av[:, sl] for sl in sls]
    yield
    g_incs = [_dot_f32(tri_f, lw) for lw in lws]
    yield
    g_lasts = [g[chunk - 1:chunk, :] for g in g_incs]
    x_mats, y_mats = [], []
    for h in heads:
        e_neg = jnp.exp(-g_incs[h])
        x_mats.append(jnp.concatenate([a_vecs[h] * jnp.exp(g_incs[h] - lws[h]), rs[h] * jnp.exp(g_incs[h])], axis=0))
        y_mats.append(jnp.concatenate([b_vecs[h] * e_neg, ks[h] * e_neg], axis=0))
    mms = [_dot_nt(x_mats[h], y_mats[h]) for h in heads]
    s0s = [state_ref[h] for h in heads]
    xss = [_dot_nt(x_mats[h], s0s[h]) for h in heads]
    yield
    rhss = [xss[h][lo] + _dot(jnp.where(strict, mms[h][lo, hi], 0.0), vs[h]) for h in heads]
    t_qs = yield from _unit_lower_inverse_minus_eye([jnp.where(strict, m[lo, lo], 0.0) for m in mms], pair_masks)
    us = [rhss[h] + _dot(t_qs[h], rhss[h]) for h in heads]
    yield
    ys = [xss[h][hi] + _dot(jnp.where(tri, mms[h][hi, lo], 0.0), us[h])
          + _dot(jnp.where(tri, mms[h][hi, hi], 0.0), vs[h]) for h in heads]
    yield
    for h in heads:
        e_tail = jnp.exp(g_lasts[h] - g_incs[h])
        state_ref[h] = (s0s[h] * jnp.exp(g_lasts[h])
                        + _dot_tn(jnp.concatenate([us[h], vs[h]], axis=0),
                                  jnp.concatenate([b_vecs[h] * e_tail, ks[h] * e_tail], axis=0)))

    outs = []
    for h in heads:
        y, sl = ys[h], sls[h]
        mean = jnp.mean(y, axis=-1, keepdims=True)
        var = jnp.mean(jnp.square(y - mean), axis=-1, keepdims=True)
        yn = (y - mean) * lax.rsqrt(var + GN_EPS_A) * ln_w[:, sl] + ln_b[:, sl]
        bonus = jnp.sum(rk_all[:, sl], axis=-1, keepdims=True) * vs[h]
        outs.append((yn + bonus) * g_out[:, sl])

    y_ref[0] = jnp.concatenate(outs, axis=-1).astype(y_ref.dtype)

    @pl.when(c == pl.num_programs(1) - 1)
    def _():
        sout_ref[0] = state_ref[...]


def _rwkv(pa3, a_shift, states_in, layer_in, state_stack, layer, lp, chunk, nseq):
    nb, t, cols = pa3.shape
    n_heads, hd = states_in.shape[2], states_in.shape[3]
    width = n_heads * hd
    tp = -(-t // chunk) * chunk
    if tp != t:
        pa3 = jnp.pad(pa3, ((0, 0), (0, tp - t), (0, 0)))
    full2 = lambda b, c: (0, 0)
    vec = lambda n: pl.BlockSpec((1, n), full2)
    return dict(
        stages=functools.partial(_rwkv_stages, chunk=chunk, t_valid=t, n_heads=n_heads, hd=hd),
        args=(pa3, a_shift.reshape(nb, 1, cols), states_in, lp["a_mu"], lp["a_w0"], lp["a_w2"], lp["a_a0"],
              lp["a_a2"], lp["a_g2"], lp["a_kk"], lp["a_ka"], lp["a_rk"], lp["a_ln_w"], lp["a_ln_b"]),
        in_kinds=(1, 1, 2) + (0,) * 11,
        out_kinds=(1, 2),
        state_stack=state_stack,
        in_specs=[pl.BlockSpec((nseq, chunk, cols), lambda b, c: (b, c, 0)),
                  pl.BlockSpec((nseq, 1, cols), lambda b, c: (b, 0, 0)),
                  pl.BlockSpec((1, nseq, n_heads, hd, hd), lambda b, c: (layer_in, b, 0, 0, 0)),
                  vec(cols), vec(width), pl.BlockSpec((LORA_W, width), full2),
                  vec(width), pl.BlockSpec((LORA_A, width), full2), pl.BlockSpec((LORA_G, width), full2),
                  vec(width), vec(width), vec(width), vec(width), vec(width)],
        out_shape=[jax.ShapeDtypeStruct((nb, tp, width), BF16), jax.ShapeDtypeStruct(state_stack.shape, F32)],
        out_specs=[pl.BlockSpec((nseq, chunk, width), lambda b, c: (b, c, 0)),
                   pl.BlockSpec((1, nseq, n_heads, hd, hd), lambda b, c: (layer, b, 0, 0, 0))],
        scratch_shapes=[pltpu.VMEM((nseq, n_heads, hd, hd), F32), pltpu.VMEM((nseq, 1, cols), F32)],
        grid=(nb // nseq, tp // chunk))


def _gdn_stages(pc_ref, cv_ref, s0_ref, cw_ref, alog_ref, dtb_ref, nw_ref,
                y_ref, sout_ref, state_ref, carry_ref, *, chunk, t_valid, n_heads, hd):
    c = pl.program_id(1)
    width = n_heads * hd
    taps = CONV_W - 1

    @pl.when(c == 0)
    def _():
        state_ref[...] = s0_ref[0]
        carry_ref[...] = cv_ref[0]

    pc = pc_ref[0]
    x = pc[:, 0:3 * width]
    cz = pc[:, 3 * width:4 * width]
    row = lax.broadcasted_iota(jnp.int32, (chunk, 1), 0)
    conv = x * cw_ref[taps:taps + 1, :]
    for j in range(1, CONV_W):
        shifted = pltpu.roll(x, j, axis=0)
        for i in range(j):
            shifted = jnp.where(row == i, carry_ref[taps - j + i:taps - j + i + 1, :], shifted)
        conv = conv + shifted * cw_ref[taps - j:taps - j + 1, :]
    carry_ref[...] = x[chunk - taps:chunk, :]
    act = jax.nn.silu(conv)
    q_all = act[:, 0:width]
    k_all = act[:, width:2 * width]
    v_all = act[:, 2 * width:3 * width]

    valid = (row + c * chunk) < t_valid
    ps = pc[:, 4 * width:4 * width + LANES]
    beta_all = jnp.where(valid, jax.nn.sigmoid(ps), 0.0)
    lane = lax.broadcasted_iota(jnp.int32, (1, ps.shape[1]), 1)
    dt_full = dtb_ref[...]
    neg_rate = -jnp.exp(alog_ref[...])
    g_all = jnp.where(valid & (lane >= n_heads) & (lane < 2 * n_heads),
                      neg_rate * jax.nn.softplus(ps + dt_full), 0.0)
    tri, strict = _tri_masks(chunk)
    pair_masks = _block_pair_masks(chunk)
    gc_all = _dot_f32(tri.astype(F32), g_all)
    yield
    gc_rows = gc_all.T

    heads = range(n_heads)
    sls = [slice(h * hd, (h + 1) * hd) for h in heads]
    lo, hi = slice(0, chunk), slice(chunk, 2 * chunk)
    qs, ks, vs, kbs, gcols, decays, e_gs, betas = [], [], [], [], [], [], [], []
    for h in heads:
        q = q_all[:, sls[h]]
        qs.append(q * lax.rsqrt(jnp.sum(q * q, axis=-1, keepdims=True) + 1e-6) * (hd ** -0.5))
        k = k_all[:, sls[h]]
        k = k * lax.rsqrt(jnp.sum(k * k, axis=-1, keepdims=True) + 1e-6)
        ks.append(k)
        vs.append(v_all[:, sls[h]])
        beta = beta_all[:, h:h + 1]
        betas.append(beta)
        kbs.append(k * beta)
        gcol = gc_all[:, n_heads + h:n_heads + h + 1]
        grow = gc_rows[n_heads + h:n_heads + h + 1, :]
        gcols.append(gcol)
        decays.append(jnp.where(tri, jnp.exp(jnp.where(tri, gcol - grow, 0.0)), 0.0))
        e_gs.append(jnp.exp(gcol))
    mms = [_dot_nt(jnp.concatenate([kbs[h], qs[h]], axis=0), ks[h]) for h in heads]
    yield
    t_qs = yield from _unit_lower_inverse_minus_eye(
        [jnp.where(strict, -(mms[h][lo] * decays[h]), 0.0) for h in heads], pair_masks)
    rhss = [jnp.concatenate([vs[h] * betas[h], kbs[h] * e_gs[h]], axis=1) for h in heads]
    uws = [rhss[h] + _dot(t_qs[h], rhss[h]) for h in heads]
    yield
    s0s = [state_ref[h] for h in heads]
    wss = [_dot(jnp.concatenate([uws[h][:, hd:2 * hd], qs[h] * e_gs[h]], axis=0), s0s[h]) for h in heads]
    yield
    v_news = [uws[h][:, 0:hd] - wss[h][lo] for h in heads]
    os_ = [wss[h][hi] + _dot(mms[h][hi] * decays[h], v_news[h]) for h in heads]
    yield
    for h in heads:
        g_last = gcols[h][chunk - 1:chunk, :]
        state_ref[h] = s0s[h] * jnp.exp(g_last) + _dot_tn(ks[h] * jnp.exp(g_last - gcols[h]), v_news[h])

    norm_w = nw_ref[...]
    outs = []
    for h in heads:
        o = os_[h]
        on = o * lax.rsqrt(jnp.mean(o * o, axis=-1, keepdims=True) + RMS_EPS) * norm_w
        outs.append(on * jax.nn.silu(cz[:, sls[h]]))

    y_ref[0] = jnp.concatenate(outs, axis=-1).astype(y_ref.dtype)

    @pl.when(c == pl.num_programs(1) - 1)
    def _():
        sout_ref[0] = state_ref[...]


def _gdn(pc3, c_conv, states_in, layer_in, state_stack, layer, lp, chunk, nseq):
    nb, t, cols = pc3.shape
    n_heads, hd = states_in.shape[2], states_in.shape[3]
    width = n_heads * hd
    taps = CONV_W - 1
    tp = -(-t // chunk) * chunk
    if tp != t:
        pc3 = jnp.pad(pc3, ((0, 0), (0, tp - t), (0, 0)))
    full2 = lambda b, c: (0, 0)
    return dict(
        stages=functools.partial(_gdn_stages, chunk=chunk, t_valid=t, n_heads=n_heads, hd=hd),
        args=(pc3, c_conv, states_in, lp["c_conv_w"], lp["c_a_log_pad"], lp["c_dt_bias_pad"], lp["c_norm_w"]),
        in_kinds=(1, 1, 2) + (0,) * 4,
        out_kinds=(1, 2),
        state_stack=state_stack,
        in_specs=[pl.BlockSpec((nseq, chunk, cols), lambda b, c: (b, c, 0)),
                  pl.BlockSpec((nseq, taps, 3 * width), lambda b, c: (b, 0, 0)),
                  pl.BlockSpec((1, nseq, n_heads, hd, hd), lambda b, c: (layer_in, b, 0, 0, 0)),
                  pl.BlockSpec((CONV_W, 3 * width), full2),
                  pl.BlockSpec((1, LANES), full2), pl.BlockSpec((1, LANES), full2),
                  pl.BlockSpec((1, hd), full2)],
        out_shape=[jax.ShapeDtypeStruct((nb, tp, width), BF16), jax.ShapeDtypeStruct(state_stack.shape, F32)],
        out_specs=[pl.BlockSpec((nseq, chunk, width), lambda b, c: (b, c, 0)),
                   pl.BlockSpec((1, nseq, n_heads, hd, hd), lambda b, c: (layer, b, 0, 0, 0))],
        scratch_shapes=[pltpu.VMEM((nseq, n_heads, hd, hd), F32), pltpu.VMEM((nseq, taps, 3 * width), F32)],
        grid=(nb // nseq, tp // chunk))


def _recurrent_kernel(*refs, parts, nseq):
    groups = []
    pos = 0
    for key in ("n_in", "n_out", "n_scratch"):
        group = []
        for part in parts:
            group.append(refs[pos:pos + part[key]])
            pos += part[key]
        groups.append(group)
        if key == "n_in":
            pos += len(parts)
    ins, outs, scratch = groups
    gens = []
    for s in range(nseq):
        one = pl.ds(s, 1)
        view = lambda r, kind: r if kind == 0 else (r.at[one] if kind == 1 else r.at[0].at[one])
        for i, part in enumerate(parts):
            gens.append(part["stages"](
                *[view(r, kind) for r, kind in zip(ins[i], part["in_kinds"])],
                *[view(r, kind) for r, kind in zip(outs[i], part["out_kinds"])],
                *[r.at[s] for r in scratch[i]]))
    _interleave(*gens)


def _recurrent_mixers(specs):
    grid = specs[0]["grid"]
    assert all(s["grid"] == grid for s in specs)
    parts = tuple(dict(stages=s["stages"], n_in=len(s["in_specs"]), n_out=len(s["out_specs"]),
                       n_scratch=len(s["scratch_shapes"]), in_kinds=s["in_kinds"], out_kinds=s["out_kinds"])
                  for s in specs)
    flat = lambda key: [item for s in specs for item in s[key]]
    nseq = specs[0]["in_specs"][0].block_shape[0]
    n_in = len(flat("in_specs"))
    state_out, pos = [], 0
    for s in specs:
        pos += len(s["out_specs"])
        state_out.append(pos - 1)
    outs = pl.pallas_call(
        functools.partial(_recurrent_kernel, parts=parts, nseq=nseq),
        out_shape=flat("out_shape"),
        grid=grid,
        in_specs=flat("in_specs") + [pl.BlockSpec(memory_space=pl.ANY)] * len(specs),
        out_specs=flat("out_specs"),
        scratch_shapes=flat("scratch_shapes"),
        input_output_aliases={n_in + i: o for i, o in enumerate(state_out)},
        compiler_params=_params("parallel", "arbitrary"),
        name="recurrent_mixers",
    )(*flat("args"), *[s["state_stack"] for s in specs])
    results, pos = [], 0
    for s in specs:
        results.append(outs[pos:pos + len(s["out_specs"])])
        pos += len(s["out_specs"])
    return results


def _t5_bucket(dist):
    n = jnp.maximum(dist, 0)
    max_exact = N_BUCKETS // 2
    nf = jnp.maximum(n, 1).astype(F32)
    large = max_exact + (jnp.log(nf / max_exact) / math.log(MAX_DISTANCE / max_exact)
                         * (N_BUCKETS - max_exact)).astype(jnp.int32)
    return jnp.where(n < max_exact, n, jnp.minimum(large, N_BUCKETS - 1))


def _diff_attn_kernel(rb_ref, sc_ref, q_ref, k_ref, v_ref, sub_ref, o_ref, bias_ref, kb_ref, vb_ref, *, blk, hd):
    h = pl.program_id(0)
    qi = pl.program_id(2)
    far_bias = rb_ref[N_BUCKETS - 1, h]

    @pl.when(qi == 0)
    def _():
        kb_ref[...] = k_ref[0].astype(BF16)
        vb_ref[...] = v_ref[0].astype(BF16)

    @pl.when((qi == 0) & (pl.program_id(1) == 0))
    def _():
        r = lax.broadcasted_iota(jnp.int32, (blk, blk), 0)
        c = lax.broadcasted_iota(jnp.int32, (blk, blk), 1)
        for delta in range(2):
            dist = delta * blk + r - c
            bucket = _t5_bucket(dist)
            bias = jnp.zeros((blk, blk), F32)
            for n in range(N_BUCKETS):
                bias = jnp.where(bucket == n, rb_ref[n, h], bias)
            bias_ref[delta] = jnp.where(dist >= 0, bias, NEG_INF)

    q = q_ref[0] * (hd ** -0.5)
    first_half = lax.broadcasted_iota(jnp.int32, (1, 2 * hd), 1) < hd
    qs = (jnp.where(first_half, q, 0.0).astype(BF16), jnp.where(first_half, 0.0, q).astype(BF16))

    def step(kj, bias, carry):
        start = pl.multiple_of(kj * blk, blk)
        kt = kb_ref[pl.ds(start, blk), :]
        vt = vb_ref[pl.ds(start, blk), :]
        maps = range(2)
        ss = [_dot_nt(qs[c], kt) + bias for c in maps]
        m_news = [jnp.maximum(carry[c][0], jnp.max(ss[c], axis=-1, keepdims=True)) for c in maps]
        ps = [jnp.exp(ss[c] - m_news[c]) for c in maps]
        pvs = [jnp.dot(ps[c].astype(BF16), vt, preferred_element_type=F32) for c in maps]
        new = []
        for c in maps:
            m, l, acc = carry[c]
            alpha = jnp.exp(m - m_news[c])
            new.append((m_news[c], alpha * l + jnp.sum(ps[c], axis=-1, keepdims=True), alpha * acc + pvs[c]))
        return tuple(new)

    init = tuple((jnp.full((blk, 1), NEG_INF, F32), jnp.zeros((blk, 1), F32), jnp.zeros((blk, 2 * hd), F32))
                 for _ in range(2))
    carry = lax.fori_loop(0, jnp.maximum(qi - 1, 0), lambda kj, cr: step(kj, far_bias, cr), init)
    carry = lax.cond(qi >= 1, lambda cr: step(qi - 1, bias_ref[1], cr), lambda cr: cr, carry)
    carry = step(qi, bias_ref[0], carry)

    (_, l1, acc1), (_, l2, acc2) = carry
    out = acc1 / l1 - sc_ref[0] * (acc2 / l2)
    out = _rms(out, sub_ref[...], SUBLN_EPS) * sc_ref[1]
    o_ref[0] = out.astype(o_ref.dtype)


def _diff_attn_prompt(pb3, rel_bias, scal, subln):
    nb, t, cols = pb3.shape
    n_heads = rel_bias.shape[1]
    hw = cols // (3 * n_heads)
    blk = next(b for b in (ATT_BLOCK, ATT_BLOCK // 2, MAX_DISTANCE) if t % b == 0)
    assert t % blk == 0 and MAX_DISTANCE <= blk
    smem = pl.BlockSpec(memory_space=pltpu.SMEM)
    return pl.pallas_call(
        functools.partial(_diff_attn_kernel, blk=blk, hd=hw // 2),
        out_shape=jax.ShapeDtypeStruct((nb, t, n_heads * hw), BF16),
        grid=(n_heads, nb, t // blk),
        in_specs=[smem, smem,
                  pl.BlockSpec((1, blk, hw), lambda h, b, i: (b, i, h)),
                  pl.BlockSpec((1, t, hw), lambda h, b, i: (b, 0, n_heads + h)),
                  pl.BlockSpec((1, t, hw), lambda h, b, i: (b, 0, 2 * n_heads + h)),
                  pl.BlockSpec((1, hw), lambda h, b, i: (0, 0))],
        out_specs=pl.BlockSpec((1, blk, hw), lambda h, b, i: (b, i, h)),
        scratch_shapes=[pltpu.VMEM((2, blk, blk), F32), pltpu.VMEM((t, hw), BF16), pltpu.VMEM((t, hw), BF16)],
        compiler_params=_params("arbitrary", "arbitrary", "arbitrary"),
        name="diff_attn_prompt",
    )(rel_bias, scal, pb3, pb3, pb3, subln)


def _paged_attn_kernel(pt_ref, sc_ref, q_ref, kn_ref, vn_ref, rb_ref, sub_ref, *rest, n_heads, hd, page, n_pages):
    k_refs = rest[0:n_pages]
    v_refs = rest[n_pages:2 * n_pages]
    o_ref, bias_ref = rest[2 * n_pages], rest[2 * n_pages + 1]
    n_maps = 2 * n_heads
    rows = page * n_heads
    map_idx = lax.broadcasted_iota(jnp.int32, (n_maps, 1), 0)

    @pl.when(pl.program_id(0) == 0)
    def _():
        lane = lax.broadcasted_iota(jnp.int32, (1, rows), 1)
        own_head = (lane % n_heads) == (map_idx // 2)
        far = jnp.broadcast_to(rb_ref[:, N_BUCKETS - 1:N_BUCKETS], (n_maps, rows))
        bucket = _t5_bucket(page - lane // n_heads)
        near = jnp.zeros((n_maps, rows), F32)
        for n in range(N_BUCKETS):
            near = jnp.where(bucket == n, rb_ref[:, n:n + 1], near)
        bias_ref[0] = jnp.where(own_head, far, NEG_INF)
        bias_ref[1] = jnp.where(own_head, near, NEG_INF)

    own_half = (lax.broadcasted_iota(jnp.int32, (1, 2 * hd), 1) // hd) == (map_idx % 2)
    qm = jnp.where(own_half, q_ref[0] * (hd ** -0.5), 0.0)
    qb = qm.astype(BF16)

    scores = [_dot_nt(qb, k_refs[i][0, 0].astype(BF16)) + bias_ref[1 if i == n_pages - 1 else 0]
              for i in range(n_pages)]
    s_self = jnp.sum(qm * kn_ref[0], axis=1, keepdims=True) + rb_ref[:, 0:1]
    m = s_self
    for s in scores:
        m = jnp.maximum(m, jnp.max(s, axis=1, keepdims=True))
    p_self = jnp.exp(s_self - m)
    l = p_self
    acc = p_self * vn_ref[0]
    for i in range(n_pages):
        p = jnp.exp(scores[i] - m)
        l = l + jnp.sum(p, axis=1, keepdims=True)
        acc = acc + jnp.dot(p.astype(BF16), v_refs[i][0, 0].astype(BF16), preferred_element_type=F32)
    o = acc / l
    d = o - sc_ref[0] * pltpu.roll(o, n_maps - 1, axis=0)
    o_ref[0] = (_rms(d, sub_ref[...], SUBLN_EPS) * sc_ref[1]).astype(o_ref.dtype)


def _diff_attn_sample(pb, cache_k, cache_v, page_table, layer, rel_bias, scal, subln):
    nbs, cols = pb.shape
    n_heads = rel_bias.shape[1]
    width = cols // 3
    hw = width // n_heads
    n_pool, depth, page = cache_k.shape[0], cache_k.shape[1], cache_k.shape[2]
    n_pages = page_table.shape[1]
    assert MAX_DISTANCE <= page
    n_maps = 2 * n_heads
    rows = page * n_heads
    ck = cache_k.reshape(n_pool, depth, rows, hw)
    cv = cache_v.reshape(n_pool, depth, rows, hw)
    qkv = jnp.repeat(pb.reshape(nbs, 3, n_heads, hw), 2, axis=2)
    rb_maps = jnp.repeat(rel_bias, 2, axis=1).T
    per_seq = pl.BlockSpec((1, n_maps, hw), lambda b, pt: (b, 0, 0))
    page_spec = lambda i: pl.BlockSpec((1, 1, rows, hw), lambda b, pt: (pt[b * n_pages + i], layer, 0, 0))
    grid_spec = pltpu.PrefetchScalarGridSpec(
        num_scalar_prefetch=1,
        grid=(nbs,),
        in_specs=[pl.BlockSpec(memory_space=pltpu.SMEM), per_seq, per_seq, per_seq,
                  pl.BlockSpec(rb_maps.shape, lambda b, pt: (0, 0)),
                  pl.BlockSpec((1, hw), lambda b, pt: (0, 0))]
                 + [page_spec(i) for i in range(n_pages)] + [page_spec(i) for i in range(n_pages)],
        out_specs=per_seq,
        scratch_shapes=[pltpu.VMEM((2, n_maps, rows), F32)])
    out = pl.pallas_call(
        functools.partial(_paged_attn_kernel, n_heads=n_heads, hd=hw // 2, page=page, n_pages=n_pages),
        out_shape=jax.ShapeDtypeStruct((nbs, n_maps, hw), BF16),
        grid_spec=grid_spec,
        compiler_params=_params("arbitrary"),
        name="diff_attn_paged",
    )(page_table.reshape(-1), scal, qkv[:, 0], qkv[:, 1], qkv[:, 2], rb_maps, subln,
      *([ck] * n_pages), *([cv] * n_pages))
    return out[:, 0::2].reshape(nbs, width)


def _mem_attn_kernel(q_ref, k_ref, v_ref, o_ref, *, n_heads, hd):
    q = q_ref[0] * (hd ** -0.5)
    heads = range(n_heads)
    qs = [q[:, h * hd:(h + 1) * hd] for h in heads]
    if len(k_ref.shape) == 3:
        k = k_ref[0]
        v = v_ref[0]
        scores = [_dot_nt(qs[h], k[:, h * hd:(h + 1) * hd]) for h in heads]
    else:
        scores = [_dot(qs[h], k_ref[0, 0, h]) for h in heads]
    probs = []
    for s in scores:
        p = jnp.exp(s - jnp.max(s, axis=-1, keepdims=True))
        probs.append(p / jnp.sum(p, axis=-1, keepdims=True))
    if len(k_ref.shape) == 3:
        outs = [_dot(probs[h], v[:, h * hd:(h + 1) * hd]) for h in heads]
    else:
        outs = [_dot_nt(probs[h], v_ref[0, 0, h]) for h in heads]
    o_ref[0] = jnp.concatenate(outs, axis=-1).astype(o_ref.dtype)


def _mem_attn(q3, mk, mv, layer, n_heads, tq=512):
    nb, t, w = q3.shape
    tq = _tile(t, tq)
    if mk.ndim == 3:
        kv_spec = pl.BlockSpec((1,) + mk.shape[1:], lambda b, i: (b, 0, 0))
    else:
        mk = jnp.transpose(mk, (0, 1, 3, 4, 2))
        mv = jnp.transpose(mv, (0, 1, 3, 4, 2))
        kv_spec = pl.BlockSpec((1, 1) + mk.shape[2:], lambda b, i: (layer, b, 0, 0, 0))
    return pl.pallas_call(
        functools.partial(_mem_attn_kernel, n_heads=n_heads, hd=w // n_heads),
        out_shape=jax.ShapeDtypeStruct((nb, t, w), BF16),
        grid=(nb, t // tq),
        in_specs=[pl.BlockSpec((1, tq, w), lambda b, i: (b, i, 0)), kv_spec, kv_spec],
        out_specs=pl.BlockSpec((1, tq, w), lambda b, i: (b, i, 0)),
        compiler_params=_params("parallel", "arbitrary"),
        name="mem_attn",
    )(q3, mk, mv)


def _step_kernel(pa_ref, sh_ref, pc_ref, cv_ref, sa_ref, sc_ref,
                 mu_ref, w0_ref, w2_ref, a0_ref, a2_ref, g2_ref, kk_ref, ka_ref, rk_ref, lnw_ref, lnb_ref,
                 cw_ref, alog_ref, dtb_ref, nw_ref, sa_alias, sc_alias,
                 ya_ref, yc_ref, sa_out, sc_out,
                 at_ref, ct_ref, pst_ref, yat_ref, yct_ref, col_ref, *, n_heads, hd):
    del sa_alias, sc_alias
    h = pl.program_id(0)
    width = n_heads * hd
    nb = pa_ref.shape[0]
    rows = pl.ds(pl.multiple_of(h * hd, hd), hd)
    colsum = lambda x: jnp.sum(x, axis=0, keepdims=True)

    pa = pa_ref[...]
    xa = pa + (sh_ref[...] - pa) * mu_ref[...]
    o = 3 * width
    lora_w = _dot(jnp.tanh(xa[:, o:o + LORA_W]), w2_ref[...])
    lora_a = _dot(xa[:, o + LORA_W:o + LORA_W + LORA_A], a2_ref[...])
    g_out = _dot(jax.nn.sigmoid(xa[:, o + LORA_W + LORA_A:o + LORA_W + LORA_A + LORA_G]), g2_ref[...])

    @pl.when(h == 0)
    def _():
        ar, ak, av = xa[:, 0:width], xa[:, width:2 * width], xa[:, 2 * width:3 * width]
        decay = jnp.exp(-jnp.exp(-jax.nn.softplus(-(w0_ref[...] + lora_w)) - 0.5))
        a_in = jax.nn.sigmoid(a0_ref[...] + lora_a)
        k_all = ak * (1.0 + (a_in - 1.0) * ka_ref[...])
        for i, arr in enumerate((ar, decay, k_all, av, ak * kk_ref[...], a_in, g_out, ar * k_all * rk_ref[...])):
            at_ref[i] = arr.T
        pc = pc_ref[...]
        taps = CONV_W - 1
        conv = pc[:, 0:3 * width] * cw_ref[taps:taps + 1, :]
        for j in range(taps):
            conv = conv + cv_ref[:, j, :] * cw_ref[j:j + 1, :]
        act = jax.nn.silu(conv)
        for i, arr in enumerate((act[:, 0:width], act[:, width:2 * width], act[:, 2 * width:3 * width],
                                 jax.nn.silu(pc[:, 3 * width:4 * width]))):
            ct_ref[i] = arr.T
        ps = pc[:, 4 * width:4 * width + LANES]
        gate = -jnp.exp(alog_ref[...]) * jax.nn.softplus(ps + dtb_ref[...])
        lane = lax.broadcasted_iota(jnp.int32, (1, LANES), 1)
        pst_ref[...] = jnp.where(lane < n_heads, jax.nn.sigmoid(ps), gate).T

    r, w, k, v = at_ref[0, rows, :], at_ref[1, rows, :], at_ref[2, rows, :], at_ref[3, rows, :]
    kk = at_ref[4, rows, :]
    kk = kk * lax.rsqrt(colsum(kk * kk) + 1e-12)
    a_vec = -kk
    b_vec = kk * at_ref[5, rows, :]
    col_ref[0] = v

    def a_row(i, carry):
        s_row = sa_ref[0, 0, i]
        s_a = colsum(s_row * a_vec)
        s_new = s_row * w + s_a * b_vec + col_ref[0, pl.ds(i, 1), :] * k
        sa_out[0, 0, i] = s_new
        col_ref[1, pl.ds(i, 1), :] = colsum(s_new * r)
        return carry

    lax.fori_loop(0, hd, a_row, 0, unroll=4)
    y = col_ref[1]
    mean = colsum(y) * (1.0 / hd)
    var = colsum(jnp.square(y - mean)) * (1.0 / hd)
    yn = (y - mean) * lax.rsqrt(var + GN_EPS_A) * lnw_ref[rows, :] + lnb_ref[rows, :]
    bonus = colsum(at_ref[7, rows, :]) * v
    yat_ref[rows, :] = (yn + bonus) * at_ref[6, rows, :]

    q, kc, vc = ct_ref[0, rows, :], ct_ref[1, rows, :], ct_ref[2, rows, :]
    q = q * lax.rsqrt(colsum(q * q) + 1e-6) * (hd ** -0.5)
    kc = kc * lax.rsqrt(colsum(kc * kc) + 1e-6)
    beta = pst_ref[pl.ds(h, 1), :]
    g = pst_ref[pl.ds(n_heads + h, 1), :]
    e_g = jnp.exp(g)
    col_ref[2] = kc * (beta * e_g)
    col_ref[3] = q * e_g
    col_ref[4] = kc

    def c_reduce(i, carry):
        t_w, t_q = carry
        s_row = sc_ref[0, 0, i]
        return (t_w + col_ref[2, pl.ds(i, 1), :] * s_row, t_q + col_ref[3, pl.ds(i, 1), :] * s_row)

    zero = jnp.zeros((hd, nb), F32)
    t_w, t_q = lax.fori_loop(0, hd, c_reduce, (zero, zero), unroll=4)
    v_new = vc * beta - t_w
    o = t_q + colsum(q * kc) * v_new

    def c_update(i, carry):
        sc_out[0, 0, i] = sc_ref[0, 0, i] * e_g + col_ref[4, pl.ds(i, 1), :] * v_new
        return carry

    lax.fori_loop(0, hd, c_update, 0, unroll=4)
    on = o * lax.rsqrt(colsum(o * o) * (1.0 / hd) + RMS_EPS) * nw_ref[...]
    yct_ref[rows, :] = on * ct_ref[3, rows, :]

    @pl.when(h == n_heads - 1)
    def _():
        ya_ref[...] = yat_ref[...].T.astype(ya_ref.dtype)
        yc_ref[...] = yct_ref[...].T.astype(yc_ref.dtype)


def _step_mixers(pa, a_shift, pc, c_conv, wkv_t, delta_t, wkv_stack, delta_stack, layer, lp):
    nb, a_cols = pa.shape
    depth, n_heads, hd = wkv_t.shape[0], wkv_t.shape[1], wkv_t.shape[2]
    width = n_heads * hd
    col = lambda v: v.reshape(-1, 1)
    params = (lp["a_mu"], lp["a_w0"], lp["a_w2"], lp["a_a0"], lp["a_a2"], lp["a_g2"], lp["a_kk"], lp["a_ka"],
              lp["a_rk"], col(lp["a_ln_w"]), col(lp["a_ln_b"]), lp["c_conv_w"], lp["c_a_log_pad"],
              lp["c_dt_bias_pad"], col(lp["c_norm_w"]))
    whole = lambda a: pl.BlockSpec(a.shape, lambda h: (0,) * a.ndim)
    state_in = pl.BlockSpec((1, 1, hd, hd, nb), lambda h: (layer, h, 0, 0, 0))
    any_spec = pl.BlockSpec(memory_space=pl.ANY)
    ya, yc, new_wkv, new_delta = pl.pallas_call(
        functools.partial(_step_kernel, n_heads=n_heads, hd=hd),
        out_shape=(jax.ShapeDtypeStruct((nb, width), BF16), jax.ShapeDtypeStruct((nb, width), BF16),
                   jax.ShapeDtypeStruct(wkv_stack.shape, F32), jax.ShapeDtypeStruct(delta_stack.shape, F32)),
        grid=(n_heads,),
        in_specs=[whole(pa), whole(a_shift), whole(pc), whole(c_conv), state_in, state_in]
                 + [whole(a) for a in params] + [any_spec, any_spec],
        out_specs=(pl.BlockSpec((nb, width), lambda h: (0, 0)), pl.BlockSpec((nb, width), lambda h: (0, 0)),
                   state_in, state_in),
        scratch_shapes=[pltpu.VMEM((8, width, nb), F32), pltpu.VMEM((4, width, nb), F32),
                        pltpu.VMEM((LANES, nb), F32), pltpu.VMEM((width, nb), F32), pltpu.VMEM((width, nb), F32),
                        pltpu.VMEM((5, hd, nb), F32)],
        input_output_aliases={6 + len(params): 2, 7 + len(params): 3},
        compiler_params=_params("arbitrary"),
        name="step_mixers",
    )(pa, a_shift, pc, c_conv, wkv_t, delta_t, *params, wkv_stack, delta_stack)
    return ya, yc, new_wkv, new_delta


def _layer(lp, layer, x, mem_k, mem_v, a_shift, a_wkv, c_conv, c_delta, paged, kv_stacks):
    nb, t, d = x.shape
    m = nb * t
    x2 = x.reshape(m, d)
    g = lp["g_pre_mix"]
    pa = _mm(x2, lp["w_in_a"], g, tn=896)
    pb, new_bk, new_bv = _mm_qkv(x2, lp["w_in_b"], g, nb, t, layer, *kv_stacks)
    pc = _mm(x2, lp["w_in_c"], g, tn=lp["w_in_c"].shape[1])
    pg = _mm(x2, lp["w_in_g"], g)

    chunk = CHUNK if t >= CHUNK else SUBLANES
    pa3 = pa.reshape(nb, t, -1)
    pc3 = pc.reshape(nb, t, -1)
    if t == 1:
        assert a_wkv[1] == layer and c_delta[1] == layer
        ya, yc, new_wkv, new_delta = _step_mixers(pa, a_shift, pc, c_conv, a_wkv[0], c_delta[0],
                                                  a_wkv[2], c_delta[2], layer, lp)
    else:
        want = SEQS_PER_STEP_LONG if t >= CHUNK else SEQS_PER_STEP_SHORT
        nseq = max(n for n in range(1, want + 1) if nb % n == 0)
        (ya, new_wkv), (yc, new_delta) = _recurrent_mixers(
            [_rwkv(pa3, a_shift, *a_wkv, layer, lp, chunk, nseq),
             _gdn(pc3, c_conv, *c_delta, layer, lp, chunk, nseq)])
        ya, yc = ya[:, :t], yc[:, :t]
    new_shift = pa3[:, t - 1]

    if paged is None:
        yb = _diff_attn_prompt(pb.reshape(nb, t, -1), lp["rel_bias"], lp["b_scal"], lp["b_subln"])
    else:
        cache_k, cache_v, page_table = paged
        yb = _diff_attn_sample(pb, cache_k, cache_v, page_table, layer, lp["rel_bias"], lp["b_scal"], lp["b_subln"])
    w_c3 = c_conv.shape[2]
    new_conv = jnp.concatenate([c_conv, pc3[:, :, 0:w_c3]], axis=1)[:, t:]

    x2 = _merge(ya.reshape(m, -1), yb.reshape(m, -1), yc.reshape(m, -1), pg, x2,
                lp["w_br_a"], lp["w_br_b"], lp["w_br_c"], lp["w_out"], lp["g_post_mix"])

    q = _mm(x2, lp["w_mem_q"], lp["g_pre_mem"])
    att = _mem_attn(q.reshape(nb, t, -1), mem_k, mem_v, layer, lp["n_heads_mem"])
    x2 = _mm_post(att.reshape(m, -1), lp["w_mem_o"], lp["g_post_mem"], x2)

    hf = _mm(x2, lp["w_ffn1"], lp["g_pre_ffn"], act="relu2", out_dtype=BF16)
    x2 = _mm_post(hf, lp["w_ffn2"], lp["g_post_ffn"], x2)
    return x2.reshape(nb, t, d), (new_shift, new_wkv, new_conv, new_delta, new_bk, new_bv)


def _layer_params(p, l):
    d = p["w_in"].shape[1]
    n_ha, hd_a = p["a_rk"].shape[1], p["a_rk"].shape[2]
    w_a = n_ha * hd_a
    a_cols = 3 * w_a + LORA_W + LORA_A + LORA_G
    w_b = p["w_br_b"].shape[1]
    w_c = p["w_br_c"].shape[1]
    n_hc = p["c_a_log"].shape[1]
    b_cols = 3 * w_b
    row = lambda v: v.reshape(1, -1).astype(F32)
    w_in = p["w_in"][l]
    o_b = a_cols
    o_c = o_b + b_cols
    o_s = o_c + 4 * w_c
    o_g = o_s + 2 * n_hc
    lam_init = 0.8 - 0.6 * math.exp(-0.3 * l)
    lam = (jnp.exp(jnp.sum(p["b_lam_q1"][l] * p["b_lam_k1"][l]))
           - jnp.exp(jnp.sum(p["b_lam_q2"][l] * p["b_lam_k2"][l])) + lam_init)
    pad_heads = lambda v: jnp.zeros((1, LANES), F32).at[0, n_hc:2 * n_hc].set(v)
    lp = {
        "w_in_a": w_in[:, 0:o_b].astype(BF16),
        "w_in_b": w_in[:, o_b:o_c].astype(BF16),
        "w_in_c": jnp.pad(w_in[:, o_c:o_g], ((0, 0), (0, LANES - 2 * n_hc))).astype(BF16),
        "w_in_g": w_in[:, o_g:o_g + 3 * d].astype(BF16),
        "a_mu": row(p["a_mu"][l]), "a_w0": row(p["a_w0"][l]), "a_w2": p["a_w2"][l].astype(BF16),
        "a_a0": row(p["a_a0"][l]), "a_a2": p["a_a2"][l].astype(BF16), "a_g2": p["a_g2"][l].astype(BF16),
        "a_kk": row(p["a_kk"][l]), "a_ka": row(p["a_ka"][l]), "a_rk": row(p["a_rk"][l]),
        "a_ln_w": row(p["a_ln_w"][l]), "a_ln_b": row(p["a_ln_b"][l]),
        "rel_bias": p["rel_bias"].astype(F32),
        "b_scal": jnp.stack([lam, jnp.asarray(1.0 - lam_init, F32)]).astype(F32),
        "b_subln": row(p["b_subln"][l]),
        "c_conv_w": p["c_conv_w"][l].astype(F32),
        "c_a_log_pad": pad_heads(p["c_a_log"][l]), "c_dt_bias_pad": pad_heads(p["c_dt_bias"][l]),
        "c_norm_w": row(p["c_norm_w"][l]),
        "w_br_a": p["w_br_a"][l].astype(BF16), "w_br_b": p["w_br_b"][l].astype(BF16),
        "w_br_c": p["w_br_c"][l].astype(BF16), "w_out": p["w_out"][l].astype(BF16),
        "g_mem": row(p["g_mem"][l]), "w_mem_q": p["w_mem_q"][l].astype(BF16),
        "w_mem_kv": p["w_mem_kv"][l].astype(BF16), "w_mem_o": p["w_mem_o"][l].astype(BF16),
        "n_heads_mem": p["n_heads_mem"],
        "w_ffn1": p["w_ffn1"][l].astype(BF16), "w_ffn2": p["w_ffn2"][l].astype(BF16),
    }
    for name in ("g_pre_mix", "g_post_mix", "g_pre_mem", "g_post_mem", "g_pre_ffn", "g_post_ffn"):
        lp[name] = row(p[name][l])
    return lp


def kernel(x_prompt, x_sample, mem_prompt, cache_b_k, cache_b_v, page_table, state_a_wkv, state_a_shift, state_c_delta, state_c_conv, cache_mem_k, cache_mem_v, g_pre_mix, g_post_mix, g_pre_mem, g_post_mem, g_pre_ffn, g_post_ffn, w_in, a_mu, a_w0, a_w2, a_a0, a_a2, a_g2, a_kk, a_ka, a_rk, a_ln_w, a_ln_b, b_lam_q1, b_lam_k1, b_lam_q2, b_lam_k2, b_subln, rel_bias, c_conv_w, c_a_log, c_dt_bias, c_norm_w, w_br_a, w_br_b, w_br_c, w_out, g_mem, w_mem_q, w_mem_kv, w_mem_o, w_ffn1, w_ffn2):
    p = dict(g_pre_mix=g_pre_mix, g_post_mix=g_post_mix, g_pre_mem=g_pre_mem, g_post_mem=g_post_mem,
             g_pre_ffn=g_pre_ffn, g_post_ffn=g_post_ffn, w_in=w_in,
             a_mu=a_mu, a_w0=a_w0, a_w2=a_w2, a_a0=a_a0, a_a2=a_a2, a_g2=a_g2, a_kk=a_kk, a_ka=a_ka,
             a_rk=a_rk, a_ln_w=a_ln_w, a_ln_b=a_ln_b,
             b_lam_q1=b_lam_q1, b_lam_k1=b_lam_k1, b_lam_q2=b_lam_q2, b_lam_k2=b_lam_k2,
             b_subln=b_subln, rel_bias=rel_bias,
             c_conv_w=c_conv_w, c_a_log=c_a_log, c_dt_bias=c_dt_bias, c_norm_w=c_norm_w,
             w_br_a=w_br_a, w_br_b=w_br_b, w_br_c=w_br_c, w_out=w_out,
             g_mem=g_mem, w_mem_q=w_mem_q, w_mem_kv=w_mem_kv, w_mem_o=w_mem_o,
             w_ffn1=w_ffn1, w_ffn2=w_ffn2)
    depth = w_in.shape[0]
    nbp = x_prompt.shape[0]
    n_ha, hd_a = a_rk.shape[1], a_rk.shape[2]
    a_cols = a_mu.shape[1]
    n_hc = c_a_log.shape[1]
    hd_c = c_norm_w.shape[1]
    n_hm = cache_mem_k.shape[3]
    p["n_heads_mem"] = n_hm
    n_mem = mem_prompt.shape[1]
    taps = c_conv_w.shape[1] - 1

    xp, xs = x_prompt, x_sample
    outs_p, outs_s, mem_kv = [], [], []
    kv_shape = lambda x: (x.shape[0], depth, x.shape[1]) + cache_b_k.shape[3:]
    kv_p = (jnp.zeros(kv_shape(xp), F32), jnp.zeros(kv_shape(xp), F32))
    kv_s = (jnp.zeros(kv_shape(xs), F32), jnp.zeros(kv_shape(xs), F32))
    zero_wkv = jnp.zeros((1, nbp, n_ha, hd_a, hd_a), F32)
    zero_delta = jnp.zeros((1, nbp, n_hc, hd_c, hd_c), F32)
    wkv_p = jnp.zeros((depth,) + zero_wkv.shape[1:], F32)
    delta_p = jnp.zeros((depth,) + zero_delta.shape[1:], F32)
    single = xs.shape[1] == 1
    to_batch_last = lambda a: jnp.transpose(a, (0, 2, 3, 4, 1)) if single else a
    from_batch_last = lambda a: jnp.transpose(a, (0, 4, 1, 2, 3)) if single else a
    wkv_in, delta_in = to_batch_last(state_a_wkv), to_batch_last(state_c_delta)
    wkv_s = jnp.zeros(wkv_in.shape, F32)
    delta_s = jnp.zeros(delta_in.shape, F32)
    for l in range(depth):
        lp = _layer_params(p, l)
        mkv = _mm(mem_prompt.reshape(nbp * n_mem, -1), lp["w_mem_kv"], lp["g_mem"])
        w_mem = mkv.shape[1] // 2
        mk = mkv[:, 0:w_mem].reshape(nbp, n_mem, w_mem)
        mv = mkv[:, w_mem:2 * w_mem].reshape(nbp, n_mem, w_mem)
        mem_kv.append((mk.reshape(nbp, n_mem, n_hm, -1), mv.reshape(nbp, n_mem, n_hm, -1)))
        xp, st = _layer(lp, l, xp, mk, mv,
                        jnp.zeros((nbp, a_cols), F32), (zero_wkv, 0, wkv_p),
                        jnp.zeros((nbp, taps, 3 * n_hc * hd_c), F32), (zero_delta, 0, delta_p),
                        None, kv_p)
        outs_p.append(st)
        kv_p, wkv_p, delta_p = st[4:6], st[1], st[3]
        xs, st = _layer(lp, l, xs, cache_mem_k, cache_mem_v,
                        state_a_shift[l], (wkv_in, l, wkv_s), state_c_conv[l], (delta_in, l, delta_s),
                        (cache_b_k, cache_b_v, page_table), kv_s)
        outs_s.append(st)
        wkv_s, delta_s = st[1], st[3]
        kv_s = st[4:6]

    stack = lambda group, i, axis: jnp.stack([st[i] for st in group], axis=axis)
    return (xp, xs, kv_p[0], kv_p[1], kv_s[0], kv_s[1],
            wkv_p, from_batch_last(wkv_s), stack(outs_p, 0, 0), stack(outs_s, 0, 0),
            delta_p, from_batch_last(delta_s), stack(outs_p, 2, 0), stack(outs_s, 2, 0),
            jnp.stack([m[0] for m in mem_kv], axis=0), jnp.stack([m[1] for m in mem_kv], axis=0))
```

```python
import functools
import math

import jax
import jax.numpy as jnp
from jax import lax
from jax.experimental import pallas as pl
from jax.experimental.pallas import tpu as pltpu

F32 = jnp.float32
BF16 = jnp.bfloat16

RMS_EPS = 1e-6
GN_EPS_A = 64e-5
SUBLN_EPS = 1e-5
N_BUCKETS = 32
MAX_DISTANCE = 128
CONV_W = 4
LORA_W = 64
LORA_A = 64
LORA_G = 128

LANES = 128
SUBLANES = 8
VMEM_LIMIT_BYTES = 56 * 1024 * 1024

CHUNK = 128
SEQS_PER_STEP_LONG = 1
SEQS_PER_STEP_SHORT = 4
ATT_BLOCK = 512
NEG_INF = -1e30


def _params(*sem):
    return pltpu.CompilerParams(dimension_semantics=sem, vmem_limit_bytes=VMEM_LIMIT_BYTES)


def _tile(n, pref):
    if n <= pref:
        return n
    for t in range(pref, 7, -1):
        if n % t == 0 and t % SUBLANES == 0:
            return t
    return n


def _dot(a, b):
    return jnp.dot(a.astype(BF16), b.astype(BF16), preferred_element_type=F32)


def _dot_nt(a, b):
    return lax.dot_general(a.astype(BF16), b.astype(BF16), (((1,), (1,)), ((), ())), preferred_element_type=F32)


def _dot_tn(a, b):
    return lax.dot_general(a.astype(BF16), b.astype(BF16), (((0,), (0,)), ((), ())), preferred_element_type=F32)


def _dot_f32(a, b):
    return jnp.dot(a, b, preferred_element_type=F32, precision=lax.Precision.HIGHEST)


def _rms(x, g, eps):
    return x * lax.rsqrt(jnp.mean(x * x, axis=-1, keepdims=True) + eps) * g


def _head_sums(x, seg):
    hi = x.astype(BF16)
    lo = (x - hi.astype(F32)).astype(BF16)
    return jnp.dot(hi, seg, preferred_element_type=F32) + jnp.dot(lo, seg, preferred_element_type=F32)


def _head_block_ones(n_heads, hd):
    idx = jnp.arange(n_heads * hd) // hd
    return (idx[:, None] == idx[None, :]).astype(BF16)


def _tri_masks(n):
    r = lax.broadcasted_iota(jnp.int32, (n, n), 0)
    c = lax.broadcasted_iota(jnp.int32, (n, n), 1)
    return r >= c, r > c


def _block_pair_masks(n):
    r = lax.broadcasted_iota(jnp.int32, (n, n), 0)
    c = lax.broadcasted_iota(jnp.int32, (n, n), 1)
    masks = []
    s = 1
    while s < n:
        masks.append(((r // (2 * s)) == (c // (2 * s))) & ((r // s) % 2 == 1) & ((c // s) % 2 == 0))
        s *= 2
    return masks


def _unit_lower_inverse_minus_eye(nmats, masks):
    qs = [jnp.where(masks[0], n, 0.0) for n in nmats]
    for mask in masks[1:]:
        n_s = [jnp.where(mask, n, 0.0) for n in nmats]
        ys = [n + _dot(q, n) for q, n in zip(qs, n_s)]
        yield
        qs = [q + y + _dot(y, q) for q, y in zip(qs, ys)]
        yield
    return qs


def _interleave(*stage_generators):
    live = list(stage_generators)
    while live:
        for gen in list(live):
            try:
                next(gen)
            except StopIteration:
                live.remove(gen)


def _mm_kernel(x_ref, g_ref, w_ref, o_ref, xn_ref, *, norm, act):
    @pl.when(pl.program_id(1) == 0)
    def _():
        if norm:
            xn_ref[...] = _rms(x_ref[...].astype(F32), g_ref[...], RMS_EPS).astype(BF16)
        else:
            xn_ref[...] = x_ref[...].astype(BF16)

    y = jnp.dot(xn_ref[...], w_ref[...], preferred_element_type=F32)
    if act == "relu2":
        y = jnp.square(jnp.maximum(y, 0.0))
    o_ref[...] = y.astype(o_ref.dtype)


def _mm(x, w, g=None, act=None, out_dtype=F32, tm=1024, tn=1024):
    m, k = x.shape
    n = w.shape[1]
    tm = _tile(m, tm)
    tn = _tile(n, tn)
    norm = g is not None
    if g is None:
        g = jnp.ones((1, k), F32)
    return pl.pallas_call(
        functools.partial(_mm_kernel, norm=norm, act=act),
        out_shape=jax.ShapeDtypeStruct((m, n), out_dtype),
        grid=(m // tm, n // tn),
        in_specs=[pl.BlockSpec((tm, k), lambda i, j: (i, 0)),
                  pl.BlockSpec((1, k), lambda i, j: (0, 0)),
                  pl.BlockSpec((k, tn), lambda i, j: (0, j))],
        out_specs=pl.BlockSpec((tm, tn), lambda i, j: (i, j)),
        scratch_shapes=[pltpu.VMEM((tm, k), BF16)],
        compiler_params=_params("parallel", "arbitrary"),
        name="mm_norm",
    )(x, g, w)


def _mm_qkv_kernel(x_ref, g_ref, w_ref, k_in, v_in, o_ref, k_out, v_out, xn_ref, *, n_heads, rows_are_tokens):
    del k_in, v_in
    j = pl.program_id(1)

    @pl.when(j == 0)
    def _():
        xn_ref[...] = _rms(x_ref[...].astype(F32), g_ref[...], RMS_EPS).astype(BF16)

    y = jnp.dot(xn_ref[...], w_ref[...], preferred_element_type=F32)
    o_ref[...] = y
    hw = y.shape[1] // n_heads

    def split_heads(dst):
        for h in range(n_heads):
            if rows_are_tokens:
                dst[0, 0, :, h, :] = y[:, h * hw:(h + 1) * hw]
            else:
                dst[:, 0, 0, h, :] = y[:, h * hw:(h + 1) * hw]

    @pl.when(j == 1)
    def _():
        split_heads(k_out)

    @pl.when(j == 2)
    def _():
        split_heads(v_out)


def _mm_qkv(x, w, g, nb, t, layer, k_stack, v_stack, tm=1024):
    m, k = x.shape
    tn = w.shape[1] // 3
    _, depth, _, n_heads, hw = k_stack.shape
    assert n_heads * hw == tn
    tm = _tile(t, tm) if t > 1 else _tile(m, tm)
    if t > 1:
        per_seq = t // tm
        kv_spec = pl.BlockSpec((1, 1, tm, n_heads, hw), lambda i, j: (i // per_seq, layer, i % per_seq, 0, 0))
    else:
        kv_spec = pl.BlockSpec((tm, 1, 1, n_heads, hw), lambda i, j: (i, layer, 0, 0, 0))
    any_spec = pl.BlockSpec(memory_space=pl.ANY)
    return pl.pallas_call(
        functools.partial(_mm_qkv_kernel, n_heads=n_heads, rows_are_tokens=t > 1),
        out_shape=(jax.ShapeDtypeStruct((m, 3 * tn), F32),
                   jax.ShapeDtypeStruct(k_stack.shape, F32), jax.ShapeDtypeStruct(v_stack.shape, F32)),
        grid=(m // tm, 3),
        in_specs=[pl.BlockSpec((tm, k), lambda i, j: (i, 0)),
                  pl.BlockSpec((1, k), lambda i, j: (0, 0)),
                  pl.BlockSpec((k, tn), lambda i, j: (0, j)),
                  any_spec, any_spec],
        out_specs=(pl.BlockSpec((tm, tn), lambda i, j: (i, j)), kv_spec, kv_spec),
        scratch_shapes=[pltpu.VMEM((tm, k), BF16)],
        input_output_aliases={3: 1, 4: 2},
        compiler_params=_params("parallel", "arbitrary"),
        name="mm_qkv",
    )(x, g, w, k_stack, v_stack)


def _mm_post_kernel(y_ref, w_ref, g_ref, res_ref, o_ref, acc_ref):
    kk = pl.program_id(1)

    @pl.when(kk == 0)
    def _():
        acc_ref[...] = jnp.zeros_like(acc_ref)

    acc_ref[...] += jnp.dot(y_ref[...].astype(BF16), w_ref[...], preferred_element_type=F32)

    @pl.when(kk == pl.num_programs(1) - 1)
    def _():
        o_ref[...] = res_ref[...] + _rms(acc_ref[...], g_ref[...], RMS_EPS)


def _mm_post(y, w, g, res, tm=512, tk=4096):
    m, k = y.shape
    n = w.shape[1]
    tm = _tile(m, tm)
    tk = _tile(k, tk)
    return pl.pallas_call(
        _mm_post_kernel,
        out_shape=jax.ShapeDtypeStruct((m, n), F32),
        grid=(m // tm, k // tk),
        in_specs=[pl.BlockSpec((tm, tk), lambda i, j: (i, j)),
                  pl.BlockSpec((tk, n), lambda i, j: (j, 0)),
                  pl.BlockSpec((1, n), lambda i, j: (0, 0)),
                  pl.BlockSpec((tm, n), lambda i, j: (i, 0))],
        out_specs=pl.BlockSpec((tm, n), lambda i, j: (i, 0)),
        scratch_shapes=[pltpu.VMEM((tm, n), F32)],
        compiler_params=_params("parallel", "arbitrary"),
        name="mm_post",
    )(y, w, g, res)


def _merge_kernel(ya_ref, yb_ref, yc_ref, pg_ref, x_ref, wa_ref, wb_ref, wc_ref, wo_ref, g_ref, o_ref):
    d = x_ref.shape[1]
    gates = jax.nn.sigmoid(pg_ref[...])
    merged = (gates[:, 0:d] * jnp.dot(ya_ref[...], wa_ref[...], preferred_element_type=F32)
              + gates[:, d:2 * d] * jnp.dot(yb_ref[...], wb_ref[...], preferred_element_type=F32)
              + gates[:, 2 * d:3 * d] * jnp.dot(yc_ref[...], wc_ref[...], preferred_element_type=F32))
    out = jnp.dot(merged.astype(BF16), wo_ref[...], preferred_element_type=F32)
    o_ref[...] = x_ref[...] + _rms(out, g_ref[...], RMS_EPS)


def _merge(ya, yb, yc, pg, x, wa, wb, wc, wo, g, tm=512):
    m, d = x.shape
    tm = _tile(m, tm)
    row = lambda i: (i, 0)
    full = lambda i: (0, 0)
    return pl.pallas_call(
        _merge_kernel,
        out_shape=jax.ShapeDtypeStruct((m, d), F32),
        grid=(m // tm,),
        in_specs=[pl.BlockSpec((tm, ya.shape[1]), row), pl.BlockSpec((tm, yb.shape[1]), row),
                  pl.BlockSpec((tm, yc.shape[1]), row), pl.BlockSpec((tm, 3 * d), row),
                  pl.BlockSpec((tm, d), row),
                  pl.BlockSpec(wa.shape, full), pl.BlockSpec(wb.shape, full), pl.BlockSpec(wc.shape, full),
                  pl.BlockSpec(wo.shape, full), pl.BlockSpec((1, d), full)],
        out_specs=pl.BlockSpec((tm, d), row),
        compiler_params=_params("parallel"),
        name="merge",
    )(ya, yb, yc, pg, x, wa, wb, wc, wo, g)


def _rwkv_stages(pa_ref, sh_ref, s0_ref, mu_ref, w0_ref, w2_ref, a0_ref, a2_ref, g2_ref, kk_ref, ka_ref, rk_ref,
                 lnw_ref, lnb_ref, seg_ref, y_ref, sout_ref, state_ref, carry_ref, *, chunk, t_valid, n_heads, hd):
    c = pl.program_id(1)
    width = n_heads * hd

    @pl.when(c == 0)
    def _():
        state_ref[...] = s0_ref[0]
        carry_ref[...] = sh_ref[0]

    pa = pa_ref[0]
    row = lax.broadcasted_iota(jnp.int32, (chunk, 1), 0)
    prev = jnp.where(row == 0, carry_ref[...], pltpu.roll(pa, 1, axis=0))
    carry_ref[...] = pa[chunk - 1:chunk, :]
    xa = pa + (prev - pa) * mu_ref[...]
    ar = xa[:, 0:width]
    ak = xa[:, width:2 * width]
    av = xa[:, 2 * width:3 * width]
    o = 3 * width
    aw = xa[:, o:o + LORA_W]
    aa = xa[:, o + LORA_W:o + LORA_W + LORA_A]
    ag = xa[:, o + LORA_W + LORA_A:o + LORA_W + LORA_A + LORA_G]

    w_raw = w0_ref[...] + _dot(jnp.tanh(aw), w2_ref[...])
    log_w = -jnp.exp(-jax.nn.softplus(-w_raw) - 0.5)
    a_in = jax.nn.sigmoid(a0_ref[...] + _dot(aa, a2_ref[...]))
    g_out = _dot(jax.nn.sigmoid(ag), g2_ref[...])
    kk_all = ak * kk_ref[...]
    kkn_all = kk_all * lax.rsqrt(_head_sums(kk_all * kk_all, seg_ref[...]) + 1e-12)
    k_all = ak * (1.0 + (a_in - 1.0) * ka_ref[...])
    rk_all = ar * k_all * rk_ref[...]

    masked = t_valid % chunk != 0
    valid = (row + c * chunk) < t_valid
    tri, strict = _tri_masks(chunk)
    pair_masks = _block_pair_masks(chunk)
    tri_f = tri.astype(F32)
    ln_w = lnw_ref[...]
    ln_b = lnb_ref[...]

    heads = range(n_heads)
    sls = [slice(h * hd, (h + 1) * hd) for h in heads]
    lo, hi = slice(0, chunk), slice(chunk, 2 * chunk)

    def prep(sl):
        kk = kkn_all[:, sl]
        k = k_all[:, sl]
        b_vec = kk * a_in[:, sl]
        lw = log_w[:, sl]
        if masked:
            k = jnp.where(valid, k, 0.0)
            b_vec = jnp.where(valid, b_vec, 0.0)
            lw = jnp.where(valid, lw, 0.0)
        return -kk, b_vec, k, lw

    a_vecs, b_vecs, ks, lws = zip(*[prep(sl) for sl in sls])
    rs = [ar[:, sl] for sl in sls]
    vs = [av[:, sl] for sl in sls]
    yield
    g_incs = [_dot_f32(tri_f, lw) for lw in lws]
    yield
    g_lasts = [g[chunk - 1:chunk, :] for g in g_incs]
    x_mats, y_mats = [], []
    for h in heads:
        e_neg = jnp.exp(-g_incs[h])
        x_mats.append(jnp.concatenate([a_vecs[h] * jnp.exp(g_incs[h] - lws[h]), rs[h] * jnp.exp(g_incs[h])], axis=0))
        y_mats.append(jnp.concatenate([b_vecs[h] * e_neg, ks[h] * e_neg], axis=0))
    mms = [_dot_nt(x_mats[h], y_mats[h]) for h in heads]
    s0s = [state_ref[h] for h in heads]
    xss = [_dot_nt(x_mats[h], s0s[h]) for h in heads]
    yield
    rhss = [xss[h][lo] + _dot(jnp.where(strict, mms[h][lo, hi], 0.0), vs[h]) for h in heads]
    t_qs = yield from _unit_lower_inverse_minus_eye([jnp.where(strict, m[lo, lo], 0.0) for m in mms], pair_masks)
    us = [rhss[h] + _dot(t_qs[h], rhss[h]) for h in heads]
    yield
    ys = [xss[h][hi] + _dot(jnp.where(tri, mms[h][hi, lo], 0.0), us[h])
          + _dot(jnp.where(tri, mms[h][hi, hi], 0.0), vs[h]) for h in heads]
    yield
    for h in heads:
        e_tail = jnp.exp(g_lasts[h] - g_incs[h])
        state_ref[h] = (s0s[h] * jnp.exp(g_lasts[h])
                        + _dot_tn(jnp.concatenate([us[h], vs[h]], axis=0),
                                  jnp.concatenate([b_vecs[h] * e_tail, ks[h] * e_tail], axis=0)))

    outs = []
    for h in heads:
        y, sl = ys[h], sls[h]
        mean = jnp.mean(y, axis=-1, keepdims=True)
        var = jnp.mean(jnp.square(y - mean), axis=-1, keepdims=True)
        yn = (y - mean) * lax.rsqrt(var + GN_EPS_A) * ln_w[:, sl] + ln_b[:, sl]
        bonus = jnp.sum(rk_all[:, sl], axis=-1, keepdims=True) * vs[h]
        outs.append((yn + bonus) * g_out[:, sl])

    y_ref[0] = jnp.concatenate(outs, axis=-1).astype(y_ref.dtype)

    @pl.when(c == pl.num_programs(1) - 1)
    def _():
        sout_ref[0] = state_ref[...]


def _rwkv(pa3, a_shift, states_in, layer_in, state_stack, layer, lp, chunk, nseq):
    nb, t, cols = pa3.shape
    n_heads, hd = states_in.shape[2], states_in.shape[3]
    width = n_heads * hd
    tp = -(-t // chunk) * chunk
    if tp != t:
        pa3 = jnp.pad(pa3, ((0, 0), (0, tp - t), (0, 0)))
    full2 = lambda b, c: (0, 0)
    vec = lambda n: pl.BlockSpec((1, n), full2)
    return dict(
        stages=functools.partial(_rwkv_stages, chunk=chunk, t_valid=t, n_heads=n_heads, hd=hd),
        args=(pa3, a_shift.reshape(nb, 1, cols), states_in, lp["a_mu"], lp["a_w0"], lp["a_w2"], lp["a_a0"],
              lp["a_a2"], lp["a_g2"], lp["a_kk"], lp["a_ka"], lp["a_rk"], lp["a_ln_w"], lp["a_ln_b"],
              _head_block_ones(n_heads, hd)),
        in_kinds=(1, 1, 2) + (0,) * 12,
        out_kinds=(1, 2),
        state_stack=state_stack,
        in_specs=[pl.BlockSpec((nseq, chunk, cols), lambda b, c: (b, c, 0)),
                  pl.BlockSpec((nseq, 1, cols), lambda b, c: (b, 0, 0)),
                  pl.BlockSpec((1, nseq, n_heads, hd, hd), lambda b, c: (layer_in, b, 0, 0, 0)),
                  vec(cols), vec(width), pl.BlockSpec((LORA_W, width), full2),
                  vec(width), pl.BlockSpec((LORA_A, width), full2), pl.BlockSpec((LORA_G, width), full2),
                  vec(width), vec(width), vec(width), vec(width), vec(width),
                  pl.BlockSpec((width, width), full2)],
        out_shape=[jax.ShapeDtypeStruct((nb, tp, width), BF16), jax.ShapeDtypeStruct(state_stack.shape, F32)],
        out_specs=[pl.BlockSpec((nseq, chunk, width), lambda b, c: (b, c, 0)),
                   pl.BlockSpec((1, nseq, n_heads, hd, hd), lambda b, c: (layer, b, 0, 0, 0))],
        scratch_shapes=[pltpu.VMEM((nseq, n_heads, hd, hd), F32), pltpu.VMEM((nseq, 1, cols), F32)],
        grid=(nb // nseq, tp // chunk))


def _gdn_stages(pc_ref, cv_ref, s0_ref, cw_ref, alog_ref, dtb_ref, nw_ref, seg_ref,
                y_ref, sout_ref, state_ref, carry_ref, *, chunk, t_valid, n_heads, hd):
    c = pl.program_id(1)
    width = n_heads * hd
    taps = CONV_W - 1

    @pl.when(c == 0)
    def _():
        state_ref[...] = s0_ref[0]
        carry_ref[...] = cv_ref[0]

    pc = pc_ref[0]
    x = pc[:, 0:3 * width]
    cz = pc[:, 3 * width:4 * width]
    row = lax.broadcasted_iota(jnp.int32, (chunk, 1), 0)
    conv = x * cw_ref[taps:taps + 1, :]
    for j in range(1, CONV_W):
        shifted = pltpu.roll(x, j, axis=0)
        for i in range(j):
            shifted = jnp.where(row == i, carry_ref[taps - j + i:taps - j + i + 1, :], shifted)
        conv = conv + shifted * cw_ref[taps - j:taps - j + 1, :]
    carry_ref[...] = x[chunk - taps:chunk, :]
    act = jax.nn.silu(conv)
    q_all = act[:, 0:width]
    k_all = act[:, width:2 * width]
    v_all = act[:, 2 * width:3 * width]

    valid = (row + c * chunk) < t_valid
    ps = pc[:, 4 * width:4 * width + LANES]
    beta_all = jnp.where(valid, jax.nn.sigmoid(ps), 0.0)
    lane = lax.broadcasted_iota(jnp.int32, (1, ps.shape[1]), 1)
    dt_full = dtb_ref[...]
    neg_rate = -jnp.exp(alog_ref[...])
    g_all = jnp.where(valid & (lane >= n_heads) & (lane < 2 * n_heads),
                      neg_rate * jax.nn.softplus(ps + dt_full), 0.0)
    tri, strict = _tri_masks(chunk)
    pair_masks = _block_pair_masks(chunk)
    gc_all = _dot_f32(tri.astype(F32), g_all)
    yield
    gc_rows = gc_all.T

    heads = range(n_heads)
    sls = [slice(h * hd, (h + 1) * hd) for h in heads]
    lo, hi = slice(0, chunk), slice(chunk, 2 * chunk)
    qs, ks, vs, kbs, gcols, decays, e_gs, betas = [], [], [], [], [], [], [], []
    seg = seg_ref[...]
    qn_all = q_all * lax.rsqrt(_head_sums(q_all * q_all, seg) + 1e-6) * (hd ** -0.5)
    kn_all = k_all * lax.rsqrt(_head_sums(k_all * k_all, seg) + 1e-6)
    for h in heads:
        qs.append(qn_all[:, sls[h]])
        k = kn_all[:, sls[h]]
        ks.append(k)
        vs.append(v_all[:, sls[h]])
        beta = beta_all[:, h:h + 1]
        betas.append(beta)
        kbs.append(k * beta)
        gcol = gc_all[:, n_heads + h:n_heads + h + 1]
        grow = gc_rows[n_heads + h:n_heads + h + 1, :]
        gcols.append(gcol)
        decays.append(jnp.where(tri, jnp.exp(jnp.where(tri, gcol - grow, 0.0)), 0.0))
        e_gs.append(jnp.exp(gcol))
    mms = [_dot_nt(jnp.concatenate([kbs[h], qs[h]], axis=0), ks[h]) for h in heads]
    yield
    t_qs = yield from _unit_lower_inverse_minus_eye(
        [jnp.where(strict, -(mms[h][lo] * decays[h]), 0.0) for h in heads], pair_masks)
    rhss = [jnp.concatenate([vs[h] * betas[h], kbs[h] * e_gs[h]], axis=1) for h in heads]
    uws = [rhss[h] + _dot(t_qs[h], rhss[h]) for h in heads]
    yield
    s0s = [state_ref[h] for h in heads]
    wss = [_dot(jnp.concatenate([uws[h][:, hd:2 * hd], qs[h] * e_gs[h]], axis=0), s0s[h]) for h in heads]
    yield
    v_news = [uws[h][:, 0:hd] - wss[h][lo] for h in heads]
    os_ = [wss[h][hi] + _dot(mms[h][hi] * decays[h], v_news[h]) for h in heads]
    yield
    for h in heads:
        g_last = gcols[h][chunk - 1:chunk, :]
        state_ref[h] = s0s[h] * jnp.exp(g_last) + _dot_tn(ks[h] * jnp.exp(g_last - gcols[h]), v_news[h])

    norm_w = nw_ref[...]
    outs = []
    for h in heads:
        o = os_[h]
        on = o * lax.rsqrt(jnp.mean(o * o, axis=-1, keepdims=True) + RMS_EPS) * norm_w
        outs.append(on * jax.nn.silu(cz[:, sls[h]]))

    y_ref[0] = jnp.concatenate(outs, axis=-1).astype(y_ref.dtype)

    @pl.when(c == pl.num_programs(1) - 1)
    def _():
        sout_ref[0] = state_ref[...]


def _gdn(pc3, c_conv, states_in, layer_in, state_stack, layer, lp, chunk, nseq):
    nb, t, cols = pc3.shape
    n_heads, hd = states_in.shape[2], states_in.shape[3]
    width = n_heads * hd
    taps = CONV_W - 1
    tp = -(-t // chunk) * chunk
    if tp != t:
        pc3 = jnp.pad(pc3, ((0, 0), (0, tp - t), (0, 0)))
    full2 = lambda b, c: (0, 0)
    return dict(
        stages=functools.partial(_gdn_stages, chunk=chunk, t_valid=t, n_heads=n_heads, hd=hd),
        args=(pc3, c_conv, states_in, lp["c_conv_w"], lp["c_a_log_pad"], lp["c_dt_bias_pad"], lp["c_norm_w"],
              _head_block_ones(n_heads, hd)),
        in_kinds=(1, 1, 2) + (0,) * 5,
        out_kinds=(1, 2),
        state_stack=state_stack,
        in_specs=[pl.BlockSpec((nseq, chunk, cols), lambda b, c: (b, c, 0)),
                  pl.BlockSpec((nseq, taps, 3 * width), lambda b, c: (b, 0, 0)),
                  pl.BlockSpec((1, nseq, n_heads, hd, hd), lambda b, c: (layer_in, b, 0, 0, 0)),
                  pl.BlockSpec((CONV_W, 3 * width), full2),
                  pl.BlockSpec((1, LANES), full2), pl.BlockSpec((1, LANES), full2),
                  pl.BlockSpec((1, hd), full2), pl.BlockSpec((width, width), full2)],
        out_shape=[jax.ShapeDtypeStruct((nb, tp, width), BF16), jax.ShapeDtypeStruct(state_stack.shape, F32)],
        out_specs=[pl.BlockSpec((nseq, chunk, width), lambda b, c: (b, c, 0)),
                   pl.BlockSpec((1, nseq, n_heads, hd, hd), lambda b, c: (layer, b, 0, 0, 0))],
        scratch_shapes=[pltpu.VMEM((nseq, n_heads, hd, hd), F32), pltpu.VMEM((nseq, taps, 3 * width), F32)],
        grid=(nb // nseq, tp // chunk))


def _recurrent_kernel(*refs, parts, nseq):
    groups = []
    pos = 0
    for key in ("n_in", "n_out", "n_scratch"):
        group = []
        for part in parts:
            group.append(refs[pos:pos + part[key]])
            pos += part[key]
        groups.append(group)
        if key == "n_in":
            pos += len(parts)
    ins, outs, scratch = groups
    gens = []
    for s in range(nseq):
        one = pl.ds(s, 1)
        view = lambda r, kind: r if kind == 0 else (r.at[one] if kind == 1 else r.at[0].at[one])
        for i, part in enumerate(parts):
            gens.append(part["stages"](
                *[view(r, kind) for r, kind in zip(ins[i], part["in_kinds"])],
                *[view(r, kind) for r, kind in zip(outs[i], part["out_kinds"])],
                *[r.at[s] for r in scratch[i]]))
    _interleave(*gens)


def _recurrent_mixers(specs):
    grid = specs[0]["grid"]
    assert all(s["grid"] == grid for s in specs)
    parts = tuple(dict(stages=s["stages"], n_in=len(s["in_specs"]), n_out=len(s["out_specs"]),
                       n_scratch=len(s["scratch_shapes"]), in_kinds=s["in_kinds"], out_kinds=s["out_kinds"])
                  for s in specs)
    flat = lambda key: [item for s in specs for item in s[key]]
    nseq = specs[0]["in_specs"][0].block_shape[0]
    n_in = len(flat("in_specs"))
    state_out, pos = [], 0
    for s in specs:
        pos += len(s["out_specs"])
        state_out.append(pos - 1)
    outs = pl.pallas_call(
        functools.partial(_recurrent_kernel, parts=parts, nseq=nseq),
        out_shape=flat("out_shape"),
        grid=grid,
        in_specs=flat("in_specs") + [pl.BlockSpec(memory_space=pl.ANY)] * len(specs),
        out_specs=flat("out_specs"),
        scratch_shapes=flat("scratch_shapes"),
        input_output_aliases={n_in + i: o for i, o in enumerate(state_out)},
        compiler_params=_params("parallel", "arbitrary"),
        name="recurrent_mixers",
    )(*flat("args"), *[s["state_stack"] for s in specs])
    results, pos = [], 0
    for s in specs:
        results.append(outs[pos:pos + len(s["out_specs"])])
        pos += len(s["out_specs"])
    return results


def _t5_bucket(dist):
    n = jnp.maximum(dist, 0)
    max_exact = N_BUCKETS // 2
    nf = jnp.maximum(n, 1).astype(F32)
    large = max_exact + (jnp.log(nf / max_exact) / math.log(MAX_DISTANCE / max_exact)
                         * (N_BUCKETS - max_exact)).astype(jnp.int32)
    return jnp.where(n < max_exact, n, jnp.minimum(large, N_BUCKETS - 1))


def _diff_attn_kernel(rb_ref, sc_ref, q_ref, k_ref, v_ref, sub_ref, o_ref, bias_ref, kb_ref, vb_ref, *, blk, hd):
    h = pl.program_id(0)
    qi = pl.program_id(2)
    far_bias = rb_ref[N_BUCKETS - 1, h]

    @pl.when(qi == 0)
    def _():
        kb_ref[...] = k_ref[0].astype(BF16)
        vb_ref[...] = v_ref[0].astype(BF16)

    @pl.when((qi == 0) & (pl.program_id(1) == 0))
    def _():
        r = lax.broadcasted_iota(jnp.int32, (blk, blk), 0)
        c = lax.broadcasted_iota(jnp.int32, (blk, blk), 1)
        for delta in range(2):
            dist = delta * blk + r - c
            bucket = _t5_bucket(dist)
            bias = jnp.zeros((blk, blk), F32)
            for n in range(N_BUCKETS):
                bias = jnp.where(bucket == n, rb_ref[n, h], bias)
            bias_ref[delta] = jnp.where(dist >= 0, bias, NEG_INF)

    q = q_ref[0] * (hd ** -0.5)
    first_half = lax.broadcasted_iota(jnp.int32, (1, 2 * hd), 1) < hd
    qs = (jnp.where(first_half, q, 0.0).astype(BF16), jnp.where(first_half, 0.0, q).astype(BF16))

    def step(kj, bias, carry):
        start = pl.multiple_of(kj * blk, blk)
        kt = kb_ref[pl.ds(start, blk), :]
        vt = vb_ref[pl.ds(start, blk), :]
        maps = range(2)
        ss = [_dot_nt(qs[c], kt) + bias for c in maps]
        m_news = [jnp.maximum(carry[c][0], jnp.max(ss[c], axis=-1, keepdims=True)) for c in maps]
        ps = [jnp.exp(ss[c] - m_news[c]) for c in maps]
        pvs = [jnp.dot(ps[c].astype(BF16), vt, preferred_element_type=F32) for c in maps]
        new = []
        for c in maps:
            m, l, acc = carry[c]
            alpha = jnp.exp(m - m_news[c])
            new.append((m_news[c], alpha * l + jnp.sum(ps[c], axis=-1, keepdims=True), alpha * acc + pvs[c]))
        return tuple(new)

    init = tuple((jnp.full((blk, 1), NEG_INF, F32), jnp.zeros((blk, 1), F32), jnp.zeros((blk, 2 * hd), F32))
                 for _ in range(2))
    carry = lax.fori_loop(0, jnp.maximum(qi - 1, 0), lambda kj, cr: step(kj, far_bias, cr), init)
    carry = lax.cond(qi >= 1, lambda cr: step(qi - 1, bias_ref[1], cr), lambda cr: cr, carry)
    carry = step(qi, bias_ref[0], carry)

    (_, l1, acc1), (_, l2, acc2) = carry
    out = acc1 / l1 - sc_ref[0] * (acc2 / l2)
    out = _rms(out, sub_ref[...], SUBLN_EPS) * sc_ref[1]
    o_ref[0] = out.astype(o_ref.dtype)


def _diff_attn_prompt(pb3, rel_bias, scal, subln):
    nb, t, cols = pb3.shape
    n_heads = rel_bias.shape[1]
    hw = cols // (3 * n_heads)
    blk = next(b for b in (ATT_BLOCK, ATT_BLOCK // 2, MAX_DISTANCE) if t % b == 0)
    assert t % blk == 0 and MAX_DISTANCE <= blk
    smem = pl.BlockSpec(memory_space=pltpu.SMEM)
    return pl.pallas_call(
        functools.partial(_diff_attn_kernel, blk=blk, hd=hw // 2),
        out_shape=jax.ShapeDtypeStruct((nb, t, n_heads * hw), BF16),
        grid=(n_heads, nb, t // blk),
        in_specs=[smem, smem,
                  pl.BlockSpec((1, blk, hw), lambda h, b, i: (b, i, h)),
                  pl.BlockSpec((1, t, hw), lambda h, b, i: (b, 0, n_heads + h)),
                  pl.BlockSpec((1, t, hw), lambda h, b, i: (b, 0, 2 * n_heads + h)),
                  pl.BlockSpec((1, hw), lambda h, b, i: (0, 0))],
        out_specs=pl.BlockSpec((1, blk, hw), lambda h, b, i: (b, i, h)),
        scratch_shapes=[pltpu.VMEM((2, blk, blk), F32), pltpu.VMEM((t, hw), BF16), pltpu.VMEM((t, hw), BF16)],
        compiler_params=_params("arbitrary", "arbitrary", "arbitrary"),
        name="diff_attn_prompt",
    )(rel_bias, scal, pb3, pb3, pb3, subln)


def _paged_attn_kernel(pt_ref, sc_ref, q_ref, kn_ref, vn_ref, rb_ref, sub_ref, *rest, n_heads, hd, page, n_pages):
    k_refs = rest[0:n_pages]
    v_refs = rest[n_pages:2 * n_pages]
    o_ref, bias_ref = rest[2 * n_pages], rest[2 * n_pages + 1]
    n_maps = 2 * n_heads
    rows = page * n_heads
    map_idx = lax.broadcasted_iota(jnp.int32, (n_maps, 1), 0)

    @pl.when(pl.program_id(0) == 0)
    def _():
        lane = lax.broadcasted_iota(jnp.int32, (1, rows), 1)
        own_head = (lane % n_heads) == (map_idx // 2)
        far = jnp.broadcast_to(rb_ref[:, N_BUCKETS - 1:N_BUCKETS], (n_maps, rows))
        bucket = _t5_bucket(page - lane // n_heads)
        near = jnp.zeros((n_maps, rows), F32)
        for n in range(N_BUCKETS):
            near = jnp.where(bucket == n, rb_ref[:, n:n + 1], near)
        bias_ref[0] = jnp.where(own_head, far, NEG_INF)
        bias_ref[1] = jnp.where(own_head, near, NEG_INF)

    own_half = (lax.broadcasted_iota(jnp.int32, (1, 2 * hd), 1) // hd) == (map_idx % 2)
    qm = jnp.where(own_half, q_ref[0] * (hd ** -0.5), 0.0)
    qb = qm.astype(BF16)

    scores = [_dot_nt(qb, k_refs[i][0, 0].astype(BF16)) + bias_ref[1 if i == n_pages - 1 else 0]
              for i in range(n_pages)]
    s_self = jnp.sum(qm * kn_ref[0], axis=1, keepdims=True) + rb_ref[:, 0:1]
    m = s_self
    for s in scores:
        m = jnp.maximum(m, jnp.max(s, axis=1, keepdims=True))
    p_self = jnp.exp(s_self - m)
    l = p_self
    acc = p_self * vn_ref[0]
    for i in range(n_pages):
        p = jnp.exp(scores[i] - m)
        l = l + jnp.sum(p, axis=1, keepdims=True)
        acc = acc + jnp.dot(p.astype(BF16), v_refs[i][0, 0].astype(BF16), preferred_element_type=F32)
    o = acc / l
    d = o - sc_ref[0] * pltpu.roll(o, n_maps - 1, axis=0)
    o_ref[0] = (_rms(d, sub_ref[...], SUBLN_EPS) * sc_ref[1]).astype(o_ref.dtype)


def _diff_attn_sample(pb, cache_k, cache_v, page_table, layer, rel_bias, scal, subln):
    nbs, cols = pb.shape
    n_heads = rel_bias.shape[1]
    width = cols // 3
    hw = width // n_heads
    n_pool, depth, page = cache_k.shape[0], cache_k.shape[1], cache_k.shape[2]
    n_pages = page_table.shape[1]
    assert MAX_DISTANCE <= page
    n_maps = 2 * n_heads
    rows = page * n_heads
    ck = cache_k.reshape(n_pool, depth, rows, hw)
    cv = cache_v.reshape(n_pool, depth, rows, hw)
    qkv = jnp.repeat(pb.reshape(nbs, 3, n_heads, hw), 2, axis=2)
    rb_maps = jnp.repeat(rel_bias, 2, axis=1).T
    per_seq = pl.BlockSpec((1, n_maps, hw), lambda b, pt: (b, 0, 0))
    page_spec = lambda i: pl.BlockSpec((1, 1, rows, hw), lambda b, pt: (pt[b * n_pages + i], layer, 0, 0))
    grid_spec = pltpu.PrefetchScalarGridSpec(
        num_scalar_prefetch=1,
        grid=(nbs,),
        in_specs=[pl.BlockSpec(memory_space=pltpu.SMEM), per_seq, per_seq, per_seq,
                  pl.BlockSpec(rb_maps.shape, lambda b, pt: (0, 0)),
                  pl.BlockSpec((1, hw), lambda b, pt: (0, 0))]
                 + [page_spec(i) for i in range(n_pages)] + [page_spec(i) for i in range(n_pages)],
        out_specs=per_seq,
        scratch_shapes=[pltpu.VMEM((2, n_maps, rows), F32)])
    out = pl.pallas_call(
        functools.partial(_paged_attn_kernel, n_heads=n_heads, hd=hw // 2, page=page, n_pages=n_pages),
        out_shape=jax.ShapeDtypeStruct((nbs, n_maps, hw), BF16),
        grid_spec=grid_spec,
        compiler_params=_params("arbitrary"),
        name="diff_attn_paged",
    )(page_table.reshape(-1), scal, qkv[:, 0], qkv[:, 1], qkv[:, 2], rb_maps, subln,
      *([ck] * n_pages), *([cv] * n_pages))
    return out[:, 0::2].reshape(nbs, width)


def _mem_attn_kernel(q_ref, k_ref, v_ref, o_ref, *, n_heads, hd):
    q = q_ref[0] * (hd ** -0.5)
    heads = range(n_heads)
    qs = [q[:, h * hd:(h + 1) * hd] for h in heads]
    if len(k_ref.shape) == 3:
        k = k_ref[0]
        v = v_ref[0]
        scores = [_dot_nt(qs[h], k[:, h * hd:(h + 1) * hd]) for h in heads]
    else:
        scores = [_dot(qs[h], k_ref[0, 0, h]) for h in heads]
    probs = []
    for s in scores:
        p = jnp.exp(s - jnp.max(s, axis=-1, keepdims=True))
        probs.append(p / jnp.sum(p, axis=-1, keepdims=True))
    if len(k_ref.shape) == 3:
        outs = [_dot(probs[h], v[:, h * hd:(h + 1) * hd]) for h in heads]
    else:
        outs = [_dot_nt(probs[h], v_ref[0, 0, h]) for h in heads]
    o_ref[0] = jnp.concatenate(outs, axis=-1).astype(o_ref.dtype)


def _mem_attn(q3, mk, mv, layer, n_heads, tq=512):
    nb, t, w = q3.shape
    tq = _tile(t, tq)
    if mk.ndim == 3:
        kv_spec = pl.BlockSpec((1,) + mk.shape[1:], lambda b, i: (b, 0, 0))
    else:
        mk = jnp.transpose(mk, (0, 1, 3, 4, 2))
        mv = jnp.transpose(mv, (0, 1, 3, 4, 2))
        kv_spec = pl.BlockSpec((1, 1) + mk.shape[2:], lambda b, i: (layer, b, 0, 0, 0))
    return pl.pallas_call(
        functools.partial(_mem_attn_kernel, n_heads=n_heads, hd=w // n_heads),
        out_shape=jax.ShapeDtypeStruct((nb, t, w), BF16),
        grid=(nb, t // tq),
        in_specs=[pl.BlockSpec((1, tq, w), lambda b, i: (b, i, 0)), kv_spec, kv_spec],
        out_specs=pl.BlockSpec((1, tq, w), lambda b, i: (b, i, 0)),
        compiler_params=_params("parallel", "arbitrary"),
        name="mem_attn",
    )(q3, mk, mv)


def _step_kernel(pa_ref, sh_ref, pc_ref, cv_ref, sa_ref, sc_ref,
                 mu_ref, w0_ref, w2_ref, a0_ref, a2_ref, g2_ref, kk_ref, ka_ref, rk_ref, lnw_ref, lnb_ref,
                 cw_ref, alog_ref, dtb_ref, nw_ref, sa_alias, sc_alias,
                 ya_ref, yc_ref, sa_out, sc_out,
                 at_ref, ct_ref, pst_ref, yat_ref, yct_ref, col_ref, *, n_heads, hd):
    del sa_alias, sc_alias
    h = pl.program_id(0)
    width = n_heads * hd
    nb = pa_ref.shape[0]
    rows = pl.ds(pl.multiple_of(h * hd, hd), hd)
    colsum = lambda x: jnp.sum(x, axis=0, keepdims=True)

    pa = pa_ref[...]
    xa = pa + (sh_ref[...] - pa) * mu_ref[...]
    o = 3 * width
    lora_w = _dot(jnp.tanh(xa[:, o:o + LORA_W]), w2_ref[...])
    lora_a = _dot(xa[:, o + LORA_W:o + LORA_W + LORA_A], a2_ref[...])
    g_out = _dot(jax.nn.sigmoid(xa[:, o + LORA_W + LORA_A:o + LORA_W + LORA_A + LORA_G]), g2_ref[...])

    @pl.when(h == 0)
    def _():
        ar, ak, av = xa[:, 0:width], xa[:, width:2 * width], xa[:, 2 * width:3 * width]
        decay = jnp.exp(-jnp.exp(-jax.nn.softplus(-(w0_ref[...] + lora_w)) - 0.5))
        a_in = jax.nn.sigmoid(a0_ref[...] + lora_a)
        k_all = ak * (1.0 + (a_in - 1.0) * ka_ref[...])
        for i, arr in enumerate((ar, decay, k_all, av, ak * kk_ref[...], a_in, g_out, ar * k_all * rk_ref[...])):
            at_ref[i] = arr.T
        pc = pc_ref[...]
        taps = CONV_W - 1
        conv = pc[:, 0:3 * width] * cw_ref[taps:taps + 1, :]
        for j in range(taps):
            conv = conv + cv_ref[:, j, :] * cw_ref[j:j + 1, :]
        act = jax.nn.silu(conv)
        for i, arr in enumerate((act[:, 0:width], act[:, width:2 * width], act[:, 2 * width:3 * width],
                                 jax.nn.silu(pc[:, 3 * width:4 * width]))):
            ct_ref[i] = arr.T
        ps = pc[:, 4 * width:4 * width + LANES]
        gate = -jnp.exp(alog_ref[...]) * jax.nn.softplus(ps + dtb_ref[...])
        lane = lax.broadcasted_iota(jnp.int32, (1, LANES), 1)
        pst_ref[...] = jnp.where(lane < n_heads, jax.nn.sigmoid(ps), gate).T

    r, w, k, v = at_ref[0, rows, :], at_ref[1, rows, :], at_ref[2, rows, :], at_ref[3, rows, :]
    kk = at_ref[4, rows, :]
    kk = kk * lax.rsqrt(colsum(kk * kk) + 1e-12)
    a_vec = -kk
    b_vec = kk * at_ref[5, rows, :]
    col_ref[0] = v

    def a_row(i, carry):
        s_row = sa_ref[0, 0, i]
        s_a = colsum(s_row * a_vec)
        s_new = s_row * w + s_a * b_vec + col_ref[0, pl.ds(i, 1), :] * k
        sa_out[0, 0, i] = s_new
        col_ref[1, pl.ds(i, 1), :] = colsum(s_new * r)
        return carry

    lax.fori_loop(0, hd, a_row, 0, unroll=4)
    y = col_ref[1]
    mean = colsum(y) * (1.0 / hd)
    var = colsum(jnp.square(y - mean)) * (1.0 / hd)
    yn = (y - mean) * lax.rsqrt(var + GN_EPS_A) * lnw_ref[rows, :] + lnb_ref[rows, :]
    bonus = colsum(at_ref[7, rows, :]) * v
    yat_ref[rows, :] = (yn + bonus) * at_ref[6, rows, :]

    q, kc, vc = ct_ref[0, rows, :], ct_ref[1, rows, :], ct_ref[2, rows, :]
    q = q * lax.rsqrt(colsum(q * q) + 1e-6) * (hd ** -0.5)
    kc = kc * lax.rsqrt(colsum(kc * kc) + 1e-6)
    beta = pst_ref[pl.ds(h, 1), :]
    g = pst_ref[pl.ds(n_heads + h, 1), :]
    e_g = jnp.exp(g)
    col_ref[2] = kc * (beta * e_g)
    col_ref[3] = q * e_g
    col_ref[4] = kc

    def c_reduce(i, carry):
        t_w, t_q = carry
        s_row = sc_ref[0, 0, i]
        return (t_w + col_ref[2, pl.ds(i, 1), :] * s_row, t_q + col_ref[3, pl.ds(i, 1), :] * s_row)

    zero = jnp.zeros((hd, nb), F32)
    t_w, t_q = lax.fori_loop(0, hd, c_reduce, (zero, zero), unroll=4)
    v_new = vc * beta - t_w
    o = t_q + colsum(q * kc) * v_new

    def c_update(i, carry):
        sc_out[0, 0, i] = sc_ref[0, 0, i] * e_g + col_ref[4, pl.ds(i, 1), :] * v_new
        return carry

    lax.fori_loop(0, hd, c_update, 0, unroll=4)
    on = o * lax.rsqrt(colsum(o * o) * (1.0 / hd) + RMS_EPS) * nw_ref[...]
    yct_ref[rows, :] = on * ct_ref[3, rows, :]

    @pl.when(h == n_heads - 1)
    def _():
        ya_ref[...] = yat_ref[...].T.astype(ya_ref.dtype)
        yc_ref[...] = yct_ref[...].T.astype(yc_ref.dtype)


def _step_mixers(pa, a_shift, pc, c_conv, wkv_t, delta_t, wkv_stack, delta_stack, layer, lp):
    nb, a_cols = pa.shape
    depth, n_heads, hd = wkv_t.shape[0], wkv_t.shape[1], wkv_t.shape[2]
    width = n_heads * hd
    col = lambda v: v.reshape(-1, 1)
    params = (lp["a_mu"], lp["a_w0"], lp["a_w2"], lp["a_a0"], lp["a_a2"], lp["a_g2"], lp["a_kk"], lp["a_ka"],
              lp["a_rk"], col(lp["a_ln_w"]), col(lp["a_ln_b"]), lp["c_conv_w"], lp["c_a_log_pad"],
              lp["c_dt_bias_pad"], col(lp["c_norm_w"]))
    whole = lambda a: pl.BlockSpec(a.shape, lambda h: (0,) * a.ndim)
    state_in = pl.BlockSpec((1, 1, hd, hd, nb), lambda h: (layer, h, 0, 0, 0))
    any_spec = pl.BlockSpec(memory_space=pl.ANY)
    ya, yc, new_wkv, new_delta = pl.pallas_call(
        functools.partial(_step_kernel, n_heads=n_heads, hd=hd),
        out_shape=(jax.ShapeDtypeStruct((nb, width), BF16), jax.ShapeDtypeStruct((nb, width), BF16),
                   jax.ShapeDtypeStruct(wkv_stack.shape, F32), jax.ShapeDtypeStruct(delta_stack.shape, F32)),
        grid=(n_heads,),
        in_specs=[whole(pa), whole(a_shift), whole(pc), whole(c_conv), state_in, state_in]
                 + [whole(a) for a in params] + [any_spec, any_spec],
        out_specs=(pl.BlockSpec((nb, width), lambda h: (0, 0)), pl.BlockSpec((nb, width), lambda h: (0, 0)),
                   state_in, state_in),
        scratch_shapes=[pltpu.VMEM((8, width, nb), F32), pltpu.VMEM((4, width, nb), F32),
                        pltpu.VMEM((LANES, nb), F32), pltpu.VMEM((width, nb), F32), pltpu.VMEM((width, nb), F32),
                        pltpu.VMEM((5, hd, nb), F32)],
        input_output_aliases={6 + len(params): 2, 7 + len(params): 3},
        compiler_params=_params("arbitrary"),
        name="step_mixers",
    )(pa, a_shift, pc, c_conv, wkv_t, delta_t, *params, wkv_stack, delta_stack)
    return ya, yc, new_wkv, new_delta


def _layer(lp, layer, x, mem_k, mem_v, a_shift, a_wkv, c_conv, c_delta, paged, kv_stacks):
    nb, t, d = x.shape
    m = nb * t
    x2 = x.reshape(m, d)
    g = lp["g_pre_mix"]
    pa = _mm(x2, lp["w_in_a"], g, tn=896)
    pb, new_bk, new_bv = _mm_qkv(x2, lp["w_in_b"], g, nb, t, layer, *kv_stacks)
    pc = _mm(x2, lp["w_in_c"], g, tn=lp["w_in_c"].shape[1])
    pg = _mm(x2, lp["w_in_g"], g)

    chunk = CHUNK if t >= CHUNK else SUBLANES
    pa3 = pa.reshape(nb, t, -1)
    pc3 = pc.reshape(nb, t, -1)
    if t == 1:
        assert a_wkv[1] == layer and c_delta[1] == layer
        ya, yc, new_wkv, new_delta = _step_mixers(pa, a_shift, pc, c_conv, a_wkv[0], c_delta[0],
                                                  a_wkv[2], c_delta[2], layer, lp)
    else:
        want = SEQS_PER_STEP_LONG if t >= CHUNK else SEQS_PER_STEP_SHORT
        nseq = max(n for n in range(1, want + 1) if nb % n == 0)
        (ya, new_wkv), (yc, new_delta) = _recurrent_mixers(
            [_rwkv(pa3, a_shift, *a_wkv, layer, lp, chunk, nseq),
             _gdn(pc3, c_conv, *c_delta, layer, lp, chunk, nseq)])
        ya, yc = ya[:, :t], yc[:, :t]
    new_shift = pa3[:, t - 1]

    if paged is None:
        yb = _diff_attn_prompt(pb.reshape(nb, t, -1), lp["rel_bias"], lp["b_scal"], lp["b_subln"])
    else:
        cache_k, cache_v, page_table = paged
        yb = _diff_attn_sample(pb, cache_k, cache_v, page_table, layer, lp["rel_bias"], lp["b_scal"], lp["b_subln"])
    w_c3 = c_conv.shape[2]
    new_conv = jnp.concatenate([c_conv, pc3[:, :, 0:w_c3]], axis=1)[:, t:]

    x2 = _merge(ya.reshape(m, -1), yb.reshape(m, -1), yc.reshape(m, -1), pg, x2,
                lp["w_br_a"], lp["w_br_b"], lp["w_br_c"], lp["w_out"], lp["g_post_mix"])

    q = _mm(x2, lp["w_mem_q"], lp["g_pre_mem"])
    att = _mem_attn(q.reshape(nb, t, -1), mem_k, mem_v, layer, lp["n_heads_mem"])
    x2 = _mm_post(att.reshape(m, -1), lp["w_mem_o"], lp["g_post_mem"], x2)

    hf = _mm(x2, lp["w_ffn1"], lp["g_pre_ffn"], act="relu2", out_dtype=BF16)
    x2 = _mm_post(hf, lp["w_ffn2"], lp["g_post_ffn"], x2)
    return x2.reshape(nb, t, d), (new_shift, new_wkv, new_conv, new_delta, new_bk, new_bv)


def _layer_params(p, l):
    d = p["w_in"].shape[1]
    n_ha, hd_a = p["a_rk"].shape[1], p["a_rk"].shape[2]
    w_a = n_ha * hd_a
    a_cols = 3 * w_a + LORA_W + LORA_A + LORA_G
    w_b = p["w_br_b"].shape[1]
    w_c = p["w_br_c"].shape[1]
    n_hc = p["c_a_log"].shape[1]
    b_cols = 3 * w_b
    row = lambda v: v.reshape(1, -1).astype(F32)
    w_in = p["w_in"][l]
    o_b = a_cols
    o_c = o_b + b_cols
    o_s = o_c + 4 * w_c
    o_g = o_s + 2 * n_hc
    lam_init = 0.8 - 0.6 * math.exp(-0.3 * l)
    lam = (jnp.exp(jnp.sum(p["b_lam_q1"][l] * p["b_lam_k1"][l]))
           - jnp.exp(jnp.sum(p["b_lam_q2"][l] * p["b_lam_k2"][l])) + lam_init)
    pad_heads = lambda v: jnp.zeros((1, LANES), F32).at[0, n_hc:2 * n_hc].set(v)
    lp = {
        "w_in_a": w_in[:, 0:o_b].astype(BF16),
        "w_in_b": w_in[:, o_b:o_c].astype(BF16),
        "w_in_c": jnp.pad(w_in[:, o_c:o_g], ((0, 0), (0, LANES - 2 * n_hc))).astype(BF16),
        "w_in_g": w_in[:, o_g:o_g + 3 * d].astype(BF16),
        "a_mu": row(p["a_mu"][l]), "a_w0": row(p["a_w0"][l]), "a_w2": p["a_w2"][l].astype(BF16),
        "a_a0": row(p["a_a0"][l]), "a_a2": p["a_a2"][l].astype(BF16), "a_g2": p["a_g2"][l].astype(BF16),
        "a_kk": row(p["a_kk"][l]), "a_ka": row(p["a_ka"][l]), "a_rk": row(p["a_rk"][l]),
        "a_ln_w": row(p["a_ln_w"][l]), "a_ln_b": row(p["a_ln_b"][l]),
        "rel_bias": p["rel_bias"].astype(F32),
        "b_scal": jnp.stack([lam, jnp.asarray(1.0 - lam_init, F32)]).astype(F32),
        "b_subln": row(p["b_subln"][l]),
        "c_conv_w": p["c_conv_w"][l].astype(F32),
        "c_a_log_pad": pad_heads(p["c_a_log"][l]), "c_dt_bias_pad": pad_heads(p["c_dt_bias"][l]),
        "c_norm_w": row(p["c_norm_w"][l]),
        "w_br_a": p["w_br_a"][l].astype(BF16), "w_br_b": p["w_br_b"][l].astype(BF16),
        "w_br_c": p["w_br_c"][l].astype(BF16), "w_out": p["w_out"][l].astype(BF16),
        "g_mem": row(p["g_mem"][l]), "w_mem_q": p["w_mem_q"][l].astype(BF16),
        "w_mem_kv": p["w_mem_kv"][l].astype(BF16), "w_mem_o": p["w_mem_o"][l].astype(BF16),
        "n_heads_mem": p["n_heads_mem"],
        "w_ffn1": p["w_ffn1"][l].astype(BF16), "w_ffn2": p["w_ffn2"][l].astype(BF16),
    }
    for name in ("g_pre_mix", "g_post_mix", "g_pre_mem", "g_post_mem", "g_pre_ffn", "g_post_ffn"):
        lp[name] = row(p[name][l])
    return lp


def kernel(x_prompt, x_sample, mem_prompt, cache_b_k, cache_b_v, page_table, state_a_wkv, state_a_shift, state_c_delta, state_c_conv, cache_mem_k, cache_mem_v, g_pre_mix, g_post_mix, g_pre_mem, g_post_mem, g_pre_ffn, g_post_ffn, w_in, a_mu, a_w0, a_w2, a_a0, a_a2, a_g2, a_kk, a_ka, a_rk, a_ln_w, a_ln_b, b_lam_q1, b_lam_k1, b_lam_q2, b_lam_k2, b_subln, rel_bias, c_conv_w, c_a_log, c_dt_bias, c_norm_w, w_br_a, w_br_b, w_br_c, w_out, g_mem, w_mem_q, w_mem_kv, w_mem_o, w_ffn1, w_ffn2):
    p = dict(g_pre_mix=g_pre_mix, g_post_mix=g_post_mix, g_pre_mem=g_pre_mem, g_post_mem=g_post_mem,
             g_pre_ffn=g_pre_ffn, g_post_ffn=g_post_ffn, w_in=w_in,
             a_mu=a_mu, a_w0=a_w0, a_w2=a_w2, a_a0=a_a0, a_a2=a_a2, a_g2=a_g2, a_kk=a_kk, a_ka=a_ka,
             a_rk=a_rk, a_ln_w=a_ln_w, a_ln_b=a_ln_b,
             b_lam_q1=b_lam_q1, b_lam_k1=b_lam_k1, b_lam_q2=b_lam_q2, b_lam_k2=b_lam_k2,
             b_subln=b_subln, rel_bias=rel_bias,
             c_conv_w=c_conv_w, c_a_log=c_a_log, c_dt_bias=c_dt_bias, c_norm_w=c_norm_w,
             w_br_a=w_br_a, w_br_b=w_br_b, w_br_c=w_br_c, w_out=w_out,
             g_mem=g_mem, w_mem_q=w_mem_q, w_mem_kv=w_mem_kv, w_mem_o=w_mem_o,
             w_ffn1=w_ffn1, w_ffn2=w_ffn2)
    depth = w_in.shape[0]
    nbp = x_prompt.shape[0]
    n_ha, hd_a = a_rk.shape[1], a_rk.shape[2]
    a_cols = a_mu.shape[1]
    n_hc = c_a_log.shape[1]
    hd_c = c_norm_w.shape[1]
    n_hm = cache_mem_k.shape[3]
    p["n_heads_mem"] = n_hm
    n_mem = mem_prompt.shape[1]
    taps = c_conv_w.shape[1] - 1

    xp, xs = x_prompt, x_sample
    outs_p, outs_s, mem_kv = [], [], []
    kv_shape = lambda x: (x.shape[0], depth, x.shape[1]) + cache_b_k.shape[3:]
    kv_p = (jnp.zeros(kv_shape(xp), F32), jnp.zeros(kv_shape(xp), F32))
    kv_s = (jnp.zeros(kv_shape(xs), F32), jnp.zeros(kv_shape(xs), F32))
    zero_wkv = jnp.zeros((1, nbp, n_ha, hd_a, hd_a), F32)
    zero_delta = jnp.zeros((1, nbp, n_hc, hd_c, hd_c), F32)
    wkv_p = jnp.zeros((depth,) + zero_wkv.shape[1:], F32)
    delta_p = jnp.zeros((depth,) + zero_delta.shape[1:], F32)
    single = xs.shape[1] == 1
    to_batch_last = lambda a: jnp.transpose(a, (0, 2, 3, 4, 1)) if single else a
    from_batch_last = lambda a: jnp.transpose(a, (0, 4, 1, 2, 3)) if single else a
    wkv_in, delta_in = to_batch_last(state_a_wkv), to_batch_last(state_c_delta)
    wkv_s = jnp.zeros(wkv_in.shape, F32)
    delta_s = jnp.zeros(delta_in.shape, F32)
    for l in range(depth):
        lp = _layer_params(p, l)
        mkv = _mm(mem_prompt.reshape(nbp * n_mem, -1), lp["w_mem_kv"], lp["g_mem"])
        w_mem = mkv.shape[1] // 2
        mk = mkv[:, 0:w_mem].reshape(nbp, n_mem, w_mem)
        mv = mkv[:, w_mem:2 * w_mem].reshape(nbp, n_mem, w_mem)
        mem_kv.append((mk.reshape(nbp, n_mem, n_hm, -1), mv.reshape(nbp, n_mem, n_hm, -1)))
        xp, st = _layer(lp, l, xp, mk, mv,
                        jnp.zeros((nbp, a_cols), F32), (zero_wkv, 0, wkv_p),
                        jnp.zeros((nbp, taps, 3 * n_hc * hd_c), F32), (zero_delta, 0, delta_p),
                        None, kv_p)
        outs_p.append(st)
        kv_p, wkv_p, delta_p = st[4:6], st[1], st[3]
        xs, st = _layer(lp, l, xs, cache_mem_k, cache_mem_v,
                        state_a_shift[l], (wkv_in, l, wkv_s), state_c_conv[l], (delta_in, l, delta_s),
                        (cache_b_k, cache_b_v, page_table), kv_s)
        outs_s.append(st)
        wkv_s, delta_s = st[1], st[3]
        kv_s = st[4:6]

    stack = lambda group, i, axis: jnp.stack([st[i] for st in group], axis=axis)
    return (xp, xs, kv_p[0], kv_p[1], kv_s[0], kv_s[1],
            wkv_p, from_batch_last(wkv_s), stack(outs_p, 0, 0), stack(outs_s, 0, 0),
            delta_p, from_batch_last(delta_s), stack(outs_p, 2, 0), stack(outs_s, 2, 0),
            jnp.stack([m[0] for m in mem_kv], axis=0), jnp.stack([m[1] for m in mem_kv], axis=0))
```
